```python
import math
import jax, jax.numpy as jnp
from jax import lax
import numpy as np

D_MODEL = 2048
BATCH = 8
SEQ = 2048
DEPTH = 2

HEAD_DIM = 128
N_HEADS_DIFF = 8
DIFF_MAP_DIM = HEAD_DIM // 2
N_HEADS_SB = 8
N_HEADS_MOBA = 8
MIX_W = 8 * HEAD_DIM
N_BRANCH = 3
MOBA_BLOCK = 256
MOBA_TOPK = 3
Q_BLOCK = 128
MOBA_Q_CHUNK = 16
D_FF = -(-8 * D_MODEL // (3 * 256)) * 256
RMS_EPS = 1e-6
COL_WIDTHS = [N_HEADS_DIFF * 2 * DIFF_MAP_DIM, N_HEADS_DIFF * 2 * DIFF_MAP_DIM, N_HEADS_DIFF * HEAD_DIM,
              N_HEADS_SB * HEAD_DIM, N_HEADS_SB * HEAD_DIM, N_HEADS_SB * HEAD_DIM,
              N_HEADS_MOBA * HEAD_DIM, N_HEADS_MOBA * HEAD_DIM, N_HEADS_MOBA * HEAD_DIM,
              N_BRANCH * D_MODEL]
N_COLS = sum(COL_WIDTHS)
COL_OFFSETS = [int(o) for o in np.cumsum(COL_WIDTHS)[:-1]]
N_ALIBI = N_HEADS_DIFF + N_HEADS_MOBA
ALIBI_ALL = 2.0 ** (-8.0 * (np.arange(N_ALIBI) + 1) / N_ALIBI)
ALIBI_DIFF = ALIBI_ALL[0::2].astype(np.float32)
ALIBI_MOBA = ALIBI_ALL[1::2].astype(np.float32)

kernel_name = 'hybrid_diff_stickbreak_moba_gated'


def rms_norm(x, g):
    xf = x.astype(jnp.float32)
    y = xf * lax.rsqrt(jnp.mean(xf * xf, axis=-1, keepdims=True) + RMS_EPS)
    return (y * g.astype(jnp.float32)).astype(x.dtype)


def _split_heads(t, n):
    b, s, _ = t.shape
    return t.reshape(b, s, n, -1).transpose(0, 2, 1, 3)


def _merge_heads(o):
    b, h, s, d = o.shape
    return o.transpose(0, 2, 1, 3).reshape(b, s, h * d)


def _blocks(q, size):
    b, h, s = q.shape[:3]
    rest = q.shape[3:]
    qb = q.reshape((b, h, s // size, size) + rest)
    return jnp.moveaxis(qb, 2, 0)


def _unblocks(o):
    nq, b, h, size, d = o.shape
    return jnp.moveaxis(o, 0, 2).reshape(b, h, nq * size, d)


def diff_attention(q, k, v, lam, subln_w, lambda_init):
    b, h, s, _, dm = q.shape
    scale = dm ** -0.5
    kpos = jnp.arange(s)
    slopes = jnp.asarray(ALIBI_DIFF)
    nq = s // Q_BLOCK

    def block(args):
        qi, i = args
        t = i * Q_BLOCK + jnp.arange(Q_BLOCK)
        sc = jnp.einsum('bhqmd,bhkmd->bhmqk', qi, k).astype(jnp.float32) * scale
        dist = (t[:, None] - kpos[None, :]).astype(jnp.float32)
        sc = sc - (slopes[:, None, None] * dist[None])[None, :, None]
        sc = jnp.where(kpos[None, :] <= t[:, None], sc, -jnp.inf)
        p = jax.nn.softmax(sc, axis=-1)
        a = p[:, :, 0] - lam * p[:, :, 1]
        return jnp.einsum('bhqk,bhkd->bhqd', a.astype(v.dtype), v)

    o = _unblocks(lax.map(block, (_blocks(q, Q_BLOCK), jnp.arange(nq))))
    o = rms_norm(o, subln_w).astype(jnp.float32) * (1.0 - lambda_init)
    return o.astype(v.dtype)


def stick_breaking_attention(q, k, v):
    b, h, s, d = q.shape
    scale = d ** -0.5
    kpos = jnp.arange(s)
    nq = s // Q_BLOCK

    def block(args):
        qi, i = args
        t = i * Q_BLOCK + jnp.arange(Q_BLOCK)
        z = jnp.einsum('bhqd,bhkd->bhqk', qi, k).astype(jnp.float32) * scale
        strict = kpos[None, :] < t[:, None]
        log_fail = jnp.where(strict, jax.nn.log_sigmoid(-z), 0.0)
        suffix = lax.cumsum(log_fail, axis=log_fail.ndim - 1, reverse=True) - log_fail
        w = jnp.where(strict, jnp.exp(jax.nn.log_sigmoid(z) + suffix), 0.0)
        return jnp.einsum('bhqk,bhkd->bhqd', w.astype(v.dtype), v)

    return _unblocks(lax.map(block, (_blocks(q, Q_BLOCK), jnp.arange(nq))))


def moba_attention(q, k, v):
    b, h, s, d = q.shape
    scale = d ** -0.5
    slopes = jnp.asarray(ALIBI_MOBA)
    nb = -(-s // MOBA_BLOCK)
    pad = nb * MOBA_BLOCK - s
    kp = jnp.pad(k, ((0, 0), (0, 0), (0, pad), (0, 0))).reshape(b, h, nb, MOBA_BLOCK, d)
    vp = jnp.pad(v, ((0, 0), (0, 0), (0, pad), (0, 0))).reshape(b, h, nb, MOBA_BLOCK, d)
    kmean = jnp.mean(kp.astype(jnp.float32), axis=3)
    k_sel = max(1, min(MOBA_TOPK, nb - 1))
    bi = jnp.arange(b)[:, None, None, None]
    hi = jnp.arange(h)[None, :, None, None]
    blk = jnp.arange(MOBA_BLOCK)
    nc = s // MOBA_Q_CHUNK

    def chunk(args):
        qi, c = args
        t = c * MOBA_Q_CHUNK + jnp.arange(MOBA_Q_CHUNK)
        own = (c * MOBA_Q_CHUNK) // MOBA_BLOCK
        gate = jnp.einsum('bhqd,bhnd->bhqn', qi.astype(jnp.float32), kmean)
        gate = jnp.where(jnp.arange(nb) < own, gate, -jnp.inf)
        _, idx = lax.top_k(gate, k_sel)
        valid = idx < own
        ksel = kp[bi, hi, idx]
        vsel = vp[bi, hi, idx]
        pos_sel = idx[..., None] * MOBA_BLOCK + blk
        s_sel = jnp.einsum('bhqd,bhqkpd->bhqkp', qi, ksel).astype(jnp.float32) * scale
        s_sel = s_sel - slopes[None, :, None, None, None] * (t[None, None, :, None, None] - pos_sel).astype(jnp.float32)
        s_sel = jnp.where(valid[..., None], s_sel, -jnp.inf)
        kown = lax.dynamic_index_in_dim(kp, own, axis=2, keepdims=False)
        vown = lax.dynamic_index_in_dim(vp, own, axis=2, keepdims=False)
        pos_own = own * MOBA_BLOCK + blk
        s_own = jnp.einsum('bhqd,bhpd->bhqp', qi, kown).astype(jnp.float32) * scale
        s_own = s_own - slopes[:, None, None] * (t[:, None] - pos_own[None, :]).astype(jnp.float32)
        s_own = jnp.where(pos_own[None, :] <= t[:, None], s_own, -jnp.inf)
        s_all = jnp.concatenate([s_sel.reshape(b, h, MOBA_Q_CHUNK, k_sel * MOBA_BLOCK), s_own], axis=-1)
        p = jax.nn.softmax(s_all, axis=-1).astype(v.dtype)
        p_sel = p[..., :k_sel * MOBA_BLOCK].reshape(b, h, MOBA_Q_CHUNK, k_sel, MOBA_BLOCK)
        p_own = p[..., k_sel * MOBA_BLOCK:]
        return (jnp.einsum('bhqkp,bhqkpd->bhqd', p_sel, vsel)
                + jnp.einsum('bhqp,bhpd->bhqd', p_own, vown))

    return _unblocks(lax.map(chunk, (_blocks(q, MOBA_Q_CHUNK), jnp.arange(nc))))


def hybrid_mixer(h, w_in, b_gate, lam_q1, lam_k1, lam_q2, lam_k2, subln_w, w_branch, w_out, lambda_init):
    b, s, _ = h.shape
    proj = h @ w_in
    qa, ka, va, qb, kb, vb, qc, kc, vc, g = jnp.split(proj, COL_OFFSETS, axis=-1)
    qa = _split_heads(qa, N_HEADS_DIFF).reshape(b, N_HEADS_DIFF, s, 2, DIFF_MAP_DIM)
    ka = _split_heads(ka, N_HEADS_DIFF).reshape(b, N_HEADS_DIFF, s, 2, DIFF_MAP_DIM)
    lam = (jnp.exp(jnp.sum(lam_q1.astype(jnp.float32) * lam_k1.astype(jnp.float32)))
           - jnp.exp(jnp.sum(lam_q2.astype(jnp.float32) * lam_k2.astype(jnp.float32))) + lambda_init)
    o_a = diff_attention(qa, ka, _split_heads(va, N_HEADS_DIFF), lam, subln_w, lambda_init)
    o_b = stick_breaking_attention(_split_heads(qb, N_HEADS_SB), _split_heads(kb, N_HEADS_SB), _split_heads(vb, N_HEADS_SB))
    o_c = moba_attention(_split_heads(qc, N_HEADS_MOBA), _split_heads(kc, N_HEADS_MOBA), _split_heads(vc, N_HEADS_MOBA))
    br = jnp.stack([_merge_heads(o_a), _merge_heads(o_b), _merge_heads(o_c)], axis=2)
    z = jnp.einsum('bsnm,nmd->bsnd', br, w_branch)
    gates = jax.nn.sigmoid((g + b_gate).reshape(b, s, N_BRANCH, D_MODEL))
    merged = jnp.sum(gates * z, axis=2)
    return merged @ w_out


def swiglu(h, w_gate, w_up, w_down):
    return (jax.nn.silu(h @ w_gate) * (h @ w_up)) @ w_down


def setup_inputs(seed: int = 0) -> dict:
    key = jax.random.key(seed)
    ks = jax.random.split(key, 16)
    f32 = jnp.float32
    nrm = lambda k, shape, sc: jax.random.normal(k, shape, f32) * sc
    out_sc = (2.0 * DEPTH) ** -0.5
    return {
        'x': nrm(ks[0], (BATCH, SEQ, D_MODEL), 1.0),
        'norm_mix_g': 1.0 + nrm(ks[1], (DEPTH, D_MODEL), 0.02),
        'norm_ffn_g': 1.0 + nrm(ks[2], (DEPTH, D_MODEL), 0.02),
        'w_in': nrm(ks[3], (DEPTH, D_MODEL, N_COLS), D_MODEL ** -0.5),
        'b_gate': nrm(ks[4], (DEPTH, N_BRANCH * D_MODEL), 0.01),
        'lam_q1': nrm(ks[5], (DEPTH, DIFF_MAP_DIM), 0.1),
        'lam_k1': nrm(ks[6], (DEPTH, DIFF_MAP_DIM), 0.1),
        'lam_q2': nrm(ks[7], (DEPTH, DIFF_MAP_DIM), 0.1),
        'lam_k2': nrm(ks[8], (DEPTH, DIFF_MAP_DIM), 0.1),
        'subln_w': 1.0 + nrm(ks[9], (DEPTH, HEAD_DIM), 0.02),
        'w_branch': nrm(ks[10], (DEPTH, N_BRANCH, MIX_W, D_MODEL), MIX_W ** -0.5),
        'w_out': nrm(ks[11], (DEPTH, D_MODEL, D_MODEL), D_MODEL ** -0.5 * out_sc),
        'w_ffn_gate': nrm(ks[12], (DEPTH, D_MODEL, D_FF), D_MODEL ** -0.5),
        'w_ffn_up': nrm(ks[13], (DEPTH, D_MODEL, D_FF), D_MODEL ** -0.5),
        'w_ffn_down': nrm(ks[14], (DEPTH, D_FF, D_MODEL), D_FF ** -0.5 * out_sc),
        'final_norm_g': 1.0 + nrm(ks[15], (D_MODEL,), 0.02),
    }


def reference(x, norm_mix_g, norm_ffn_g, w_in, b_gate, lam_q1, lam_k1, lam_q2, lam_k2, subln_w,
              w_branch, w_out, w_ffn_gate, w_ffn_up, w_ffn_down, final_norm_g):
    for l in range(DEPTH):
        lambda_init = 0.8 - 0.6 * math.exp(-0.3 * l)
        h = rms_norm(x, norm_mix_g[l])
        x = x + hybrid_mixer(h, w_in[l], b_gate[l], lam_q1[l], lam_k1[l], lam_q2[l], lam_k2[l],
                             subln_w[l], w_branch[l], w_out[l], lambda_init)
        h = rms_norm(x, norm_ffn_g[l])
        x = x + swiglu(h, w_ffn_gate[l], w_ffn_up[l], w_ffn_down[l])
    return rms_norm(x, final_norm_g)
```

```python
import functools
import math

import jax
import jax.numpy as jnp
import numpy as np
from jax import lax
from jax.experimental import pallas as pl
from jax.experimental.pallas import tpu as pltpu

F32 = jnp.float32
BF16 = jnp.bfloat16

HEAD_DIM = 128
N_HEADS = 8
MIX_W = N_HEADS * HEAD_DIM
N_BRANCH = 3
MOBA_BLOCK = 256
MOBA_TOPK = 3
RMS_EPS = 1e-6
N_ALIBI = 2 * N_HEADS
_ALIBI_ALL = 2.0 ** (-8.0 * (np.arange(N_ALIBI) + 1) / N_ALIBI)
ALIBI_DIFF = _ALIBI_ALL[0::2].astype(np.float32)
ALIBI_MOBA = _ALIBI_ALL[1::2].astype(np.float32)

ATT_TILE = 256
V7X_VMEM_LIMIT_BYTES = 56 * 1024 * 1024
MASKED = -1e30


def _tile(n, target):
    best = None
    for c in range(128, min(n, target) + 1, 128):
        if n % c == 0:
            best = c
    assert best is not None, (n, target)
    return best


def _dot(a, b):
    return jnp.dot(a, b, preferred_element_type=F32)


def _dot_nt(a, b):
    return lax.dot_general(a, b, (((1,), (1,)), ((), ())), preferred_element_type=F32)


def _rms(x, g):
    ms = jnp.mean(x * x, axis=-1, keepdims=True)
    return x * lax.rsqrt(ms + RMS_EPS) * g


def _norm_matmul_kernel(x_ref, g_ref, w_ref, o_ref, h_ref):
    @pl.when(pl.program_id(1) == 0)
    def _():
        h_ref[...] = _rms(x_ref[...], g_ref[...]).astype(BF16)

    o_ref[...] = _dot(h_ref[...], w_ref[...]).astype(o_ref.dtype)


def _norm_matmul(x, g, w, out_dtype, *, tm, tn, name):
    m, d = x.shape
    n = w.shape[1]
    tm, tn = _tile(m, tm), _tile(n, tn)
    return pl.pallas_call(
        _norm_matmul_kernel,
        out_shape=jax.ShapeDtypeStruct((m, n), out_dtype),
        grid=(m // tm, n // tn),
        in_specs=[
            pl.BlockSpec((tm, d), lambda i, j: (i, 0)),
            pl.BlockSpec((1, d), lambda i, j: (0, 0)),
            pl.BlockSpec((d, tn), lambda i, j: (0, j)),
        ],
        out_specs=pl.BlockSpec((tm, tn), lambda i, j: (i, j)),
        scratch_shapes=[pltpu.VMEM((tm, d), BF16)],
        compiler_params=pltpu.CompilerParams(
            dimension_semantics=("parallel", "arbitrary"),
            vmem_limit_bytes=V7X_VMEM_LIMIT_BYTES),
        name=name,
    )(x, g.reshape(1, d), w)


def _tile_dist(rows, cols, row_period):
    r = lax.broadcasted_iota(jnp.int32, (rows, cols), 0)
    c = lax.broadcasted_iota(jnp.int32, (rows, cols), 1)
    if row_period != rows:
        r = r % row_period
    return (r - c).astype(F32)


def _attn_specs(seq, col_q, col_k, col_v):
    nq = seq // ATT_TILE
    return [
        pl.BlockSpec((ATT_TILE, HEAD_DIM), lambda b, h, i: (b * nq + i, col_q + h)),
        pl.BlockSpec((seq, HEAD_DIM), lambda b, h, i: (b, col_k + h)),
        pl.BlockSpec((seq, HEAD_DIM), lambda b, h, i: (b, col_v + h)),
    ]


def _attn_out_spec(seq):
    nq = seq // ATT_TILE
    return pl.BlockSpec((ATT_TILE, HEAD_DIM), lambda b, h, i: (b * nq + i, h))


def _diff_kernel(slopes_ref, lq1_ref, lk1_ref, lq2_ref, lk2_ref, subw_ref,
                 q_ref, k_ref, v_ref, o_ref, *, lambda_init):
    t = ATT_TILE
    h = pl.program_id(1)
    i = pl.program_id(2)
    slope = slopes_ref[h]
    scale = (HEAD_DIM // 2) ** -0.5

    q = q_ref[...]
    lane = lax.broadcasted_iota(jnp.int32, q.shape, 1)
    zero = jnp.zeros_like(q)
    first = lane < HEAD_DIM // 2
    qq = jnp.concatenate([jnp.where(first, q, zero), jnp.where(first, zero, q)], axis=0)
    dist = _tile_dist(2 * t, t, t)

    def scores(j):
        kj = k_ref[pl.ds(pl.multiple_of(j * t, t), t), :]
        off = ((i - j) * t).astype(F32)
        return _dot_nt(qq, kj) * scale - slope * (dist + off)

    def pv(p, j):
        vj = v_ref[pl.ds(pl.multiple_of(j * t, t), t), :]
        return _dot(p.astype(BF16), vj)

    s = jnp.where(dist >= 0, scores(i), -jnp.inf)
    m = jnp.max(s, axis=1, keepdims=True)
    p = jnp.exp(s - m)
    l = jnp.sum(p, axis=1, keepdims=True)
    acc = pv(p, i)

    def body(j, carry):
        m, l, acc = carry
        s = scores(j)
        m_new = jnp.maximum(m, jnp.max(s, axis=1, keepdims=True))
        alpha = jnp.exp(m - m_new)
        p = jnp.exp(s - m_new)
        l = alpha * l + jnp.sum(p, axis=1, keepdims=True)
        acc = alpha * acc + pv(p, j)
        return m_new, l, acc

    m, l, acc = lax.fori_loop(0, i, body, (m, l, acc))
    o = acc / l
    lam = (jnp.exp(jnp.sum(lq1_ref[...] * lk1_ref[...], axis=1, keepdims=True))
           - jnp.exp(jnp.sum(lq2_ref[...] * lk2_ref[...], axis=1, keepdims=True))
           + lambda_init)
    o = o[:t] - lam * o[t:]
    o = _rms(o, subw_ref[...]) * (1.0 - lambda_init)
    o_ref[...] = o.astype(o_ref.dtype)


def _diff_attention(qkv, lq1, lk1, lq2, lk2, subw, lambda_init, batch, seq):
    half = HEAD_DIM // 2
    small = pl.BlockSpec((1, half), lambda b, h, i: (0, 0))
    return pl.pallas_call(
        functools.partial(_diff_kernel, lambda_init=lambda_init),
        out_shape=jax.ShapeDtypeStruct((batch * seq, MIX_W), BF16),
        grid=(batch, N_HEADS, seq // ATT_TILE),
        in_specs=[pl.BlockSpec(memory_space=pltpu.SMEM), small, small, small, small,
                  pl.BlockSpec((1, HEAD_DIM), lambda b, h, i: (0, 0))]
                 + _attn_specs(seq, 0, N_HEADS, 2 * N_HEADS),
        out_specs=_attn_out_spec(seq),
        compiler_params=pltpu.CompilerParams(
            dimension_semantics=("parallel", "parallel", "arbitrary"),
            vmem_limit_bytes=V7X_VMEM_LIMIT_BYTES),
        name="diff_attention",
    )(jnp.asarray(ALIBI_DIFF), lq1.reshape(1, half), lk1.reshape(1, half),
      lq2.reshape(1, half), lk2.reshape(1, half), subw.reshape(1, HEAD_DIM), qkv, qkv, qkv)


def _sb_kernel(q_ref, k_ref, v_ref, o_ref):
    t = ATT_TILE
    i = pl.program_id(2)
    scale = HEAD_DIM ** -0.5
    q = q_ref[...]
    r = lax.broadcasted_iota(jnp.int32, (t, t), 0)
    c = lax.broadcasted_iota(jnp.int32, (t, t), 1)
    strict = c < r
    later = (r > c).astype(BF16)

    def tile(j, run, diag):
        kj = k_ref[pl.ds(pl.multiple_of(j * t, t), t), :]
        vj = v_ref[pl.ds(pl.multiple_of(j * t, t), t), :]
        z = _dot_nt(q, kj) * scale
        sp = jnp.log1p(jnp.exp(-jnp.abs(z)))
        log_beta = jnp.minimum(z, 0.0) - sp
        log_fail = -jnp.maximum(z, 0.0) - sp
        if diag:
            log_fail = jnp.where(strict, log_fail, 0.0)
        hi = log_fail.astype(BF16)
        lo = (log_fail - hi.astype(F32)).astype(BF16)
        suffix = _dot(hi, later) + _dot(lo, later)
        w = jnp.exp(log_beta + suffix + run)
        if diag:
            w = jnp.where(strict, w, 0.0)
        contrib = _dot(w.astype(BF16), vj)
        run = run + jnp.sum(log_fail, axis=1, keepdims=True)
        return contrib, run

    acc, run = tile(i, jnp.zeros((t, 1), F32), True)

    def body(step, carry):
        acc, run = carry
        contrib, run = tile(i - 1 - step, run, False)
        return acc + contrib, run

    acc, run = lax.fori_loop(0, i, body, (acc, run))
    o_ref[...] = acc.astype(o_ref.dtype)


def _sb_attention(qkv, batch, seq):
    return pl.pallas_call(
        _sb_kernel,
        out_shape=jax.ShapeDtypeStruct((batch * seq, MIX_W), BF16),
        grid=(batch, N_HEADS, seq // ATT_TILE),
        in_specs=_attn_specs(seq, 3 * N_HEADS, 4 * N_HEADS, 5 * N_HEADS),
        out_specs=_attn_out_spec(seq),
        compiler_params=pltpu.CompilerParams(
            dimension_semantics=("parallel", "parallel", "arbitrary"),
            vmem_limit_bytes=V7X_VMEM_LIMIT_BYTES),
        name="stick_breaking_attention",
    )(qkv, qkv, qkv)


def _moba_kernel(slopes_ref, q_ref, k_ref, v_ref, o_ref, kmean_ref, selb_ref):
    t = ATT_TILE
    nb = k_ref.shape[0] // t
    h = pl.program_id(1)
    i = pl.program_id(2)
    slope = slopes_ref[h]
    scale = HEAD_DIM ** -0.5

    @pl.when(i == 0)
    def _():
        kf = k_ref[...].astype(F32).reshape(nb, t, HEAD_DIM)
        kmean_ref[...] = jnp.mean(kf, axis=1)

    q = q_ref[...]
    km = kmean_ref[...]
    km_hi = km.astype(BF16)
    km_lo = (km - km_hi.astype(F32)).astype(BF16)
    gate = _dot_nt(q, km_hi) + _dot_nt(q, km_lo)

    n_idx = lax.broadcasted_iota(jnp.int32, (t, nb), 1)
    beaten = jnp.zeros((t, nb), jnp.int32)
    for mth in range(nb):
        gm = gate[:, mth:mth + 1]
        beats = (gm > gate) | ((gm == gate) & (mth < n_idx))
        beaten = beaten + jnp.where(beats & (mth < i), 1, 0)
    sel = (n_idx < i) & (beaten < MOBA_TOPK)
    bias = jnp.where(sel, 0.0, MASKED)
    for n in range(nb):
        selb_ref[n] = jnp.broadcast_to(bias[:, n:n + 1], (t, HEAD_DIM))

    dist = _tile_dist(t, t, t)

    def scores(j):
        kj = k_ref[pl.ds(pl.multiple_of(j * t, t), t), :]
        off = ((i - j) * t).astype(F32)
        return _dot_nt(q, kj) * scale - slope * (dist + off)

    def pv(p, j):
        vj = v_ref[pl.ds(pl.multiple_of(j * t, t), t), :]
        return _dot(p.astype(BF16), vj)

    s = jnp.where(dist >= 0, scores(i), -jnp.inf)
    m = jnp.max(s, axis=1, keepdims=True)
    p = jnp.exp(s - m)
    l = jnp.sum(p, axis=1, keepdims=True)
    acc = pv(p, i)

    def body(j, carry):
        m, l, acc = carry
        b = selb_ref[j]
        s = scores(j) + jnp.concatenate([b] * (t // HEAD_DIM), axis=1)
        m_new = jnp.maximum(m, jnp.max(s, axis=1, keepdims=True))
        alpha = jnp.exp(m - m_new)
        p = jnp.exp(s - m_new)
        l = alpha * l + jnp.sum(p, axis=1, keepdims=True)
        acc = alpha * acc + pv(p, j)
        return m_new, l, acc

    m, l, acc = lax.fori_loop(0, i, body, (m, l, acc))
    o_ref[...] = (acc / l).astype(o_ref.dtype)


def _moba_attention(qkv, batch, seq):
    assert ATT_TILE == MOBA_BLOCK and seq % MOBA_BLOCK == 0
    nb = seq // MOBA_BLOCK
    return pl.pallas_call(
        _moba_kernel,
        out_shape=jax.ShapeDtypeStruct((batch * seq, MIX_W), BF16),
        grid=(batch, N_HEADS, seq // ATT_TILE),
        in_specs=[pl.BlockSpec(memory_space=pltpu.SMEM)]
                 + _attn_specs(seq, 6 * N_HEADS, 7 * N_HEADS, 8 * N_HEADS),
        out_specs=_attn_out_spec(seq),
        scratch_shapes=[pltpu.VMEM((nb, HEAD_DIM), F32),
                        pltpu.VMEM((nb, ATT_TILE, HEAD_DIM), F32)],
        compiler_params=pltpu.CompilerParams(
            dimension_semantics=("parallel", "parallel", "arbitrary"),
            vmem_limit_bytes=V7X_VMEM_LIMIT_BYTES),
        name="moba_attention",
    )(jnp.asarray(ALIBI_MOBA), qkv, qkv, qkv)


def _merge_kernel(oa_ref, ob_ref, oc_ref, ga_ref, gb_ref, gc_ref, ba_ref, bb_ref, bc_ref,
                  wbr_ref, wout_ref, x_ref, o_ref, acc_ref):
    j = pl.program_id(1)

    @pl.when(j == 0)
    def _():
        acc_ref[...] = jnp.zeros_like(acc_ref)

    merged = None
    for n, (o_r, g_r, b_r) in enumerate(((oa_ref, ga_ref, ba_ref), (ob_ref, gb_ref, bb_ref),
                                         (oc_ref, gc_ref, bc_ref))):
        z = _dot(o_r[...], wbr_ref[n])
        term = jax.nn.sigmoid(g_r[...] + b_r[...]) * z
        merged = term if merged is None else merged + term
    acc_ref[...] += _dot(merged.astype(BF16), wout_ref[...])

    @pl.when(j == pl.num_programs(1) - 1)
    def _():
        o_ref[...] = x_ref[...] + acc_ref[...]


def _merge(oa, ob, oc, gates, b_gate, w_branch, w_out, x, *, tm, tn):
    m, d = x.shape
    tm, tn = _tile(m, tm), _tile(d, tn)
    nj = d // tn
    o_spec = pl.BlockSpec((tm, MIX_W), lambda i, j: (i, 0))
    g_specs = [pl.BlockSpec((tm, tn), functools.partial(lambda i, j, n: (i, n * nj + j), n=n))
               for n in range(N_BRANCH)]
    b_specs = [pl.BlockSpec((1, tn), functools.partial(lambda i, j, n: (0, n * nj + j), n=n))
               for n in range(N_BRANCH)]
    return pl.pallas_call(
        _merge_kernel,
        out_shape=jax.ShapeDtypeStruct((m, d), F32),
        grid=(m // tm, nj),
        in_specs=[o_spec, o_spec, o_spec] + g_specs + b_specs + [
            pl.BlockSpec((N_BRANCH, MIX_W, tn), lambda i, j: (0, 0, j)),
            pl.BlockSpec((tn, d), lambda i, j: (j, 0)),
            pl.BlockSpec((tm, d), lambda i, j: (i, 0)),
        ],
        out_specs=pl.BlockSpec((tm, d), lambda i, j: (i, 0)),
        scratch_shapes=[pltpu.VMEM((tm, d), F32)],
        compiler_params=pltpu.CompilerParams(
            dimension_semantics=("parallel", "arbitrary"),
            vmem_limit_bytes=V7X_VMEM_LIMIT_BYTES),
        name="gated_merge_out_proj",
    )(oa, ob, oc, gates, gates, gates, b_gate, b_gate, b_gate, w_branch, w_out, x)


def _ffn_kernel(*refs, final_norm):
    if final_norm:
        x_ref, g_ref, wg_ref, wu_ref, wd_ref, fg_ref, o_ref, h_ref, acc_ref = refs
    else:
        x_ref, g_ref, wg_ref, wu_ref, wd_ref, o_ref, h_ref, acc_ref = refs
    j = pl.program_id(1)

    @pl.when(j == 0)
    def _():
        h_ref[...] = _rms(x_ref[...], g_ref[...]).astype(BF16)
        acc_ref[...] = jnp.zeros_like(acc_ref)

    h = h_ref[...]
    a = jax.nn.silu(_dot(h, wg_ref[...])) * _dot(h, wu_ref[...])
    acc_ref[...] += _dot(a.astype(BF16), wd_ref[...])

    @pl.when(j == pl.num_programs(1) - 1)
    def _():
        y = x_ref[...] + acc_ref[...]
        if final_norm:
            y = _rms(y, fg_ref[...])
        o_ref[...] = y


def _ffn(x, g, wg, wu, wd, final_g, *, tm, tf):
    m, d = x.shape
    dff = wg.shape[1]
    tm, tf = _tile(m, tm), _tile(dff, tf)
    vec = pl.BlockSpec((1, d), lambda i, j: (0, 0))
    in_specs = [
        pl.BlockSpec((tm, d), lambda i, j: (i, 0)),
        vec,
        pl.BlockSpec((d, tf), lambda i, j: (0, j)),
        pl.BlockSpec((d, tf), lambda i, j: (0, j)),
        pl.BlockSpec((tf, d), lambda i, j: (j, 0)),
    ]
    args = [x, g.reshape(1, d), wg, wu, wd]
    if final_g is not None:
        in_specs.append(vec)
        args.append(final_g.reshape(1, d))
    return pl.pallas_call(
        functools.partial(_ffn_kernel, final_norm=final_g is not None),
        out_shape=jax.ShapeDtypeStruct((m, d), F32),
        grid=(m // tm, dff // tf),
        in_specs=in_specs,
        out_specs=pl.BlockSpec((tm, d), lambda i, j: (i, 0)),
        scratch_shapes=[pltpu.VMEM((tm, d), BF16), pltpu.VMEM((tm, d), F32)],
        compiler_params=pltpu.CompilerParams(
            dimension_semantics=("parallel", "arbitrary"),
            vmem_limit_bytes=V7X_VMEM_LIMIT_BYTES),
        name="swiglu_ffn",
    )(*args)


def kernel(x, norm_mix_g, norm_ffn_g, w_in, b_gate, lam_q1, lam_k1, lam_q2, lam_k2, subln_w,
           w_branch, w_out, w_ffn_gate, w_ffn_up, w_ffn_down, final_norm_g):
    batch, seq, d = x.shape
    depth = w_in.shape[0]
    n_qkv = 3 * N_BRANCH * MIX_W
    assert w_in.shape[2] == n_qkv + N_BRANCH * d and seq % ATT_TILE == 0
    xf = x.reshape(batch * seq, d)
    for l in range(depth):
        lambda_init = 0.8 - 0.6 * math.exp(-0.3 * l)
        w_in_l = w_in[l].astype(BF16)
        qkv = _norm_matmul(xf, norm_mix_g[l], w_in_l[:, :n_qkv], BF16, tm=1024, tn=1024,
                           name="norm_qkv_proj")
        gates = _norm_matmul(xf, norm_mix_g[l], w_in_l[:, n_qkv:], F32, tm=1024, tn=1024,
                             name="norm_gate_proj")
        oa = _diff_attention(qkv, lam_q1[l], lam_k1[l], lam_q2[l], lam_k2[l], subln_w[l],
                             lambda_init, batch, seq)
        ob = _sb_attention(qkv, batch, seq)
        oc = _moba_attention(qkv, batch, seq)
        xf = _merge(oa, ob, oc, gates, b_gate[l].reshape(1, -1), w_branch[l].astype(BF16),
                    w_out[l].astype(BF16), xf, tm=512, tn=512)
        xf = _ffn(xf, norm_ffn_g[l], w_ffn_gate[l].astype(BF16), w_ffn_up[l].astype(BF16),
                  w_ffn_down[l].astype(BF16), final_norm_g if l == depth - 1 else None,
                  tm=512, tf=512)
    return xf.reshape(batch, seq, d)
```

```python
import functools
import math

import jax
import jax.numpy as jnp
import numpy as np
from jax import lax
from jax.experimental import pallas as pl
from jax.experimental.pallas import tpu as pltpu

F32 = jnp.float32
BF16 = jnp.bfloat16

HEAD_DIM = 128
N_HEADS = 8
MIX_W = N_HEADS * HEAD_DIM
N_BRANCH = 3
MOBA_BLOCK = 256
MOBA_TOPK = 3
RMS_EPS = 1e-6
N_ALIBI = 2 * N_HEADS
_ALIBI_ALL = 2.0 ** (-8.0 * (np.arange(N_ALIBI) + 1) / N_ALIBI)
ALIBI_DIFF = _ALIBI_ALL[0::2].astype(np.float32)
ALIBI_MOBA = _ALIBI_ALL[1::2].astype(np.float32)

LOG2E = 1.4426950408889634
V7X_VMEM_LIMIT_BYTES = 56 * 1024 * 1024
MASKED = -1e30


def _tile(n, target):
    best = None
    for c in range(128, min(n, target) + 1, 128):
        if n % c == 0:
            best = c
    assert best is not None, (n, target)
    return best


def _dot(a, b):
    return jnp.dot(a, b, preferred_element_type=F32)


def _dot_nt(a, b):
    return lax.dot_general(a, b, (((1,), (1,)), ((), ())), preferred_element_type=F32)


def _rms(x, g):
    ms = jnp.mean(x * x, axis=-1, keepdims=True)
    return x * lax.rsqrt(ms + RMS_EPS) * g


def _norm_matmul_kernel(x_ref, g_ref, w_ref, o_ref, h_ref):
    @pl.when(pl.program_id(1) == 0)
    def _():
        h_ref[...] = _rms(x_ref[...], g_ref[...]).astype(BF16)

    o_ref[...] = _dot(h_ref[...], w_ref[...]).astype(o_ref.dtype)


def _norm_matmul(x, g, w, out_dtype, *, tm, tn, name):
    m, d = x.shape
    n = w.shape[1]
    tm, tn = _tile(m, tm), _tile(n, tn)
    return pl.pallas_call(
        _norm_matmul_kernel,
        out_shape=jax.ShapeDtypeStruct((m, n), out_dtype),
        grid=(m // tm, n // tn),
        in_specs=[
            pl.BlockSpec((tm, d), lambda i, j: (i, 0)),
            pl.BlockSpec((1, d), lambda i, j: (0, 0)),
            pl.BlockSpec((d, tn), lambda i, j: (0, j)),
        ],
        out_specs=pl.BlockSpec((tm, tn), lambda i, j: (i, j)),
        scratch_shapes=[pltpu.VMEM((tm, d), BF16)],
        compiler_params=pltpu.CompilerParams(
            dimension_semantics=("parallel", "arbitrary"),
            vmem_limit_bytes=V7X_VMEM_LIMIT_BYTES),
        name=name,
    )(x, g.reshape(1, d), w)


def _attn_specs(seq, tq, hb, col_q, col_k, col_v):
    nq = seq // tq
    w = hb * HEAD_DIM
    return [
        pl.BlockSpec((tq, w), lambda b, g, i: (b * nq + i, col_q // hb + g)),
        pl.BlockSpec((seq, w), lambda b, g, i: (b, col_k // hb + g)),
        pl.BlockSpec((seq, w), lambda b, g, i: (b, col_v // hb + g)),
    ]


def _attn_out_spec(seq, tq, hb):
    nq = seq // tq
    return pl.BlockSpec((tq, hb * HEAD_DIM), lambda b, g, i: (b * nq + i, g))


_ATTN_PARAMS = pltpu.CompilerParams(
    dimension_semantics=("parallel", "parallel", "arbitrary"),
    vmem_limit_bytes=V7X_VMEM_LIMIT_BYTES)


def _head(ref, s, rows=slice(None)):
    return ref[rows, s * HEAD_DIM:(s + 1) * HEAD_DIM]


def _transpose_v(v_ref, vt_ref, hb):
    for s in range(hb):
        vt_ref[s] = _head(v_ref, s).T


def _key_query_index(tk, n, tq):
    c = lax.broadcasted_iota(jnp.int32, (tk, n), 0)
    r = lax.broadcasted_iota(jnp.int32, (tk, n), 1)
    if n != tq:
        r = r % tq
    return c, r


def _softmax_tiles(i, tq, tk, n, vt_ref, scores, stat_bias=None):
    n_full = i * (tq // tk)
    kidx, qidx = _key_query_index(tk, n, tq)
    heads = range(len(scores))

    def tiles(j, carry, masked):
        k0 = pl.multiple_of(j * tk, tk)
        us = [scores[s](k0) for s in heads]
        ps = []
        for s in heads:
            m, l, acc = carry[s]
            u, off = us[s]
            if masked:
                u = jnp.where(kidx + (k0 - i * tq) <= qidx, u, -jnp.inf)
            if stat_bias is not None:
                off = off + stat_bias[s](k0)
            m_new = jnp.maximum(m, jnp.max(u, axis=0, keepdims=True) + off)
            p = jnp.exp2(u - (m_new - off))
            alpha = jnp.exp2(m - m_new)
            l = alpha * l + jnp.sum(p, axis=0, keepdims=True)
            ps.append((m_new, l, alpha, p.astype(BF16)))
        out = []
        for s in heads:
            m_new, l, alpha, p = ps[s]
            acc = alpha * carry[s][2] + _dot(vt_ref[s, :, pl.ds(k0, tk)], p)
            out.append((m_new, l, acc))
        return tuple(out)

    carry = tuple((jnp.full((1, n), -jnp.inf, F32), jnp.zeros((1, n), F32),
                   jnp.zeros((HEAD_DIM, n), F32)) for _ in heads)
    for d in range(tq // tk):
        carry = tiles(n_full + d, carry, True)
    carry = lax.fori_loop(0, n_full, lambda j, c: tiles(j, c, False), carry)
    return [(acc, l) for _, l, acc in carry]


def _diff_kernel(slopes_ref, lq1_ref, lk1_ref, lq2_ref, lk2_ref, subw_ref,
                 q_ref, k_ref, v_ref, o_ref, vt_ref, *, lambda_init, tq, tk, hb):
    g = pl.program_id(1)
    i = pl.program_id(2)

    @pl.when(i == 0)
    def _():
        _transpose_v(v_ref, vt_ref, hb)

    half = HEAD_DIM // 2
    n = 2 * tq
    key_pos = lax.broadcasted_iota(jnp.int32, (tk, n), 0).astype(F32)
    lane = lax.broadcasted_iota(jnp.int32, (tq, HEAD_DIM), 1)

    def make_scores(s):
        slope2 = slopes_ref[g * hb + s] * LOG2E
        q = _head(q_ref, s) * jnp.asarray(half ** -0.5, BF16)
        zero = jnp.zeros_like(q)
        qq = jnp.concatenate([jnp.where(lane < half, q, zero), jnp.where(lane < half, zero, q)],
                             axis=0)
        key_bias = slope2 * key_pos

        def scores(k0):
            u = _dot_nt(_head(k_ref, s, pl.ds(k0, tk)), qq) * LOG2E + key_bias
            return u, slope2 * (k0 - i * tq).astype(F32)
        return scores

    outs = _softmax_tiles(i, tq, tk, n, vt_ref, [make_scores(s) for s in range(hb)])
    lam = (jnp.exp(jnp.sum(lq1_ref[...] * lk1_ref[...], axis=1, keepdims=True))
           - jnp.exp(jnp.sum(lq2_ref[...] * lk2_ref[...], axis=1, keepdims=True))
           + lambda_init)
    for s, (acc, l) in enumerate(outs):
        o = acc / l
        o = (o[:, :tq] - lam * o[:, tq:]).T
        o = _rms(o, subw_ref[...]) * (1.0 - lambda_init)
        o_ref[:, s * HEAD_DIM:(s + 1) * HEAD_DIM] = o.astype(o_ref.dtype)


def _diff_attention(qkv, lq1, lk1, lq2, lk2, subw, lambda_init, batch, seq, *, tq, tk, hb):
    half = HEAD_DIM // 2
    small = pl.BlockSpec((1, half), lambda b, g, i: (0, 0))
    return pl.pallas_call(
        functools.partial(_diff_kernel, lambda_init=lambda_init, tq=tq, tk=tk, hb=hb),
        out_shape=jax.ShapeDtypeStruct((batch * seq, MIX_W), BF16),
        grid=(batch, N_HEADS // hb, seq // tq),
        in_specs=[pl.BlockSpec(memory_space=pltpu.SMEM), small, small, small, small,
                  pl.BlockSpec((1, HEAD_DIM), lambda b, g, i: (0, 0))]
                 + _attn_specs(seq, tq, hb, 0, N_HEADS, 2 * N_HEADS),
        out_specs=_attn_out_spec(seq, tq, hb),
        scratch_shapes=[pltpu.VMEM((hb, HEAD_DIM, seq), BF16)],
        compiler_params=_ATTN_PARAMS,
        name="diff_attention",
    )(jnp.asarray(ALIBI_DIFF), lq1.reshape(1, half), lk1.reshape(1, half),
      lq2.reshape(1, half), lk2.reshape(1, half), subw.reshape(1, HEAD_DIM), qkv, qkv, qkv)


def _sb_kernel(q_ref, k_ref, v_ref, o_ref, vt_ref, *, tq, tk, hb):
    i = pl.program_id(2)

    @pl.when(i == 0)
    def _():
        _transpose_v(v_ref, vt_ref, hb)

    scale = HEAD_DIM ** -0.5
    n_full = i * (tq // tk)
    kidx, qidx = _key_query_index(tk, tq, tq)
    a = lax.broadcasted_iota(jnp.int32, (tk, tk), 0)
    b = lax.broadcasted_iota(jnp.int32, (tk, tk), 1)
    later = (b > a).astype(BF16)
    qs = [_head(q_ref, s) for s in range(hb)]
    heads = range(hb)

    def tiles(j, carry, masked):
        k0 = pl.multiple_of(j * tk, tk)
        zs = [_dot_nt(_head(k_ref, s, pl.ds(k0, tk)), qs[s]) * scale for s in heads]
        if masked:
            strict = kidx + (k0 - i * tq) < qidx
        mid = []
        for s in heads:
            z = zs[s]
            sp = jnp.log1p(jnp.exp(-jnp.abs(z)))
            log_beta = jnp.minimum(z, 0.0) - sp
            log_fail = -jnp.maximum(z, 0.0) - sp
            if masked:
                log_fail = jnp.where(strict, log_fail, 0.0)
            hi = log_fail.astype(BF16)
            lo = (log_fail - hi.astype(F32)).astype(BF16)
            mid.append((log_beta, log_fail, hi, lo))
        sufs = [_dot(later, mid[s][2]) + _dot(later, mid[s][3]) for s in heads]
        ws = []
        for s in heads:
            w = jnp.exp(mid[s][0] + sufs[s] + carry[s][1])
            if masked:
                w = jnp.where(strict, w, 0.0)
            ws.append(w.astype(BF16))
        out = []
        for s in heads:
            acc = carry[s][0] + _dot(vt_ref[s, :, pl.ds(k0, tk)], ws[s])
            run = carry[s][1] + jnp.sum(mid[s][1], axis=0, keepdims=True)
            out.append((acc, run))
        return tuple(out)

    carry = tuple((jnp.zeros((HEAD_DIM, tq), F32), jnp.zeros((1, tq), F32)) for _ in heads)
    for d in reversed(range(tq // tk)):
        carry = tiles(n_full + d, carry, True)
    carry = lax.fori_loop(0, n_full, lambda t, c: tiles(n_full - 1 - t, c, False), carry)
    for s in heads:
        o_ref[:, s * HEAD_DIM:(s + 1) * HEAD_DIM] = carry[s][0].T.astype(o_ref.dtype)


def _sb_attention(qkv, batch, seq, *, tq, tk, hb):
    return pl.pallas_call(
        functools.partial(_sb_kernel, tq=tq, tk=tk, hb=hb),
        out_shape=jax.ShapeDtypeStruct((batch * seq, MIX_W), BF16),
        grid=(batch, N_HEADS // hb, seq // tq),
        in_specs=_attn_specs(seq, tq, hb, 3 * N_HEADS, 4 * N_HEADS, 5 * N_HEADS),
        out_specs=_attn_out_spec(seq, tq, hb),
        scratch_shapes=[pltpu.VMEM((hb, HEAD_DIM, seq), BF16)],
        compiler_params=_ATTN_PARAMS,
        name="stick_breaking_attention",
    )(qkv, qkv, qkv)


def _moba_kernel(slopes_ref, q_ref, k_ref, v_ref, o_ref, vt_ref, kmean_ref, selb_ref,
                 *, tq, tk, hb):
    nb = k_ref.shape[0] // MOBA_BLOCK
    g = pl.program_id(1)
    i = pl.program_id(2)
    own = (i * tq) // MOBA_BLOCK

    @pl.when(i == 0)
    def _():
        _transpose_v(v_ref, vt_ref, hb)
        for s in range(hb):
            kf = _head(k_ref, s).astype(F32).reshape(nb, MOBA_BLOCK, HEAD_DIM)
            kmean_ref[s] = jnp.mean(kf, axis=1)

    c1 = HEAD_DIM ** -0.5 * LOG2E
    n_idx = lax.broadcasted_iota(jnp.int32, (nb, tq), 0)
    key_pos = lax.broadcasted_iota(jnp.int32, (tk, tq), 0).astype(F32)

    def make_head(s):
        slope2 = slopes_ref[g * hb + s] * LOG2E
        q = _head(q_ref, s)
        km = kmean_ref[s]
        km_hi = km.astype(BF16)
        km_lo = (km - km_hi.astype(F32)).astype(BF16)
        gate = _dot_nt(km_hi, q) + _dot_nt(km_lo, q)
        beaten = jnp.zeros((nb, tq), jnp.int32)
        for mth in range(nb):
            gm = gate[mth:mth + 1, :]
            beats = (gm > gate) | ((gm == gate) & (mth < n_idx))
            beaten = beaten + jnp.where(beats & (mth < own), 1, 0)
        visible = (n_idx == own) | ((n_idx < own) & (beaten < MOBA_TOPK))
        selb_ref[s] = jnp.where(visible, 0.0, MASKED)
        key_bias = slope2 * key_pos

        def scores(k0):
            u = _dot_nt(_head(k_ref, s, pl.ds(k0, tk)), q) * c1 + key_bias
            return u, slope2 * (k0 - i * tq).astype(F32)

        def block_bias(k0):
            return selb_ref[s, pl.ds(k0 // MOBA_BLOCK, 1), :]
        return scores, block_bias

    fns = [make_head(s) for s in range(hb)]
    outs = _softmax_tiles(i, tq, tk, tq, vt_ref, [f[0] for f in fns], [f[1] for f in fns])
    for s, (acc, l) in enumerate(outs):
        o_ref[:, s * HEAD_DIM:(s + 1) * HEAD_DIM] = (acc / l).T.astype(o_ref.dtype)


def _moba_attention(qkv, batch, seq, *, tq, tk, hb):
    assert seq % MOBA_BLOCK == 0 and MOBA_BLOCK % tk == 0 and MOBA_BLOCK % tq == 0
    nb = seq // MOBA_BLOCK
    return pl.pallas_call(
        functools.partial(_moba_kernel, tq=tq, tk=tk, hb=hb),
        out_shape=jax.ShapeDtypeStruct((batch * seq, MIX_W), BF16),
        grid=(batch, N_HEADS // hb, seq // tq),
        in_specs=[pl.BlockSpec(memory_space=pltpu.SMEM)]
                 + _attn_specs(seq, tq, hb, 6 * N_HEADS, 7 * N_HEADS, 8 * N_HEADS),
        out_specs=_attn_out_spec(seq, tq, hb),
        scratch_shapes=[pltpu.VMEM((hb, HEAD_DIM, seq), BF16),
                        pltpu.VMEM((hb, nb, HEAD_DIM), F32),
                        pltpu.VMEM((hb, nb, tq), F32)],
        compiler_params=_ATTN_PARAMS,
        name="moba_attention",
    )(jnp.asarray(ALIBI_MOBA), qkv, qkv, qkv)


def _merge_kernel(oa_ref, ob_ref, oc_ref, ga_ref, gb_ref, gc_ref, ba_ref, bb_ref, bc_ref,
                  wbr_ref, wout_ref, x_ref, o_ref, acc_ref):
    j = pl.program_id(1)

    @pl.when(j == 0)
    def _():
        acc_ref[...] = jnp.zeros_like(acc_ref)

    merged = None
    for n, (o_r, g_r, b_r) in enumerate(((oa_ref, ga_ref, ba_ref), (ob_ref, gb_ref, bb_ref),
                                         (oc_ref, gc_ref, bc_ref))):
        z = _dot(o_r[...], wbr_ref[n])
        term = jax.nn.sigmoid(g_r[...] + b_r[...]) * z
        merged = term if merged is None else merged + term
    acc_ref[...] += _dot(merged.astype(BF16), wout_ref[...])

    @pl.when(j == pl.num_programs(1) - 1)
    def _():
        o_ref[...] = x_ref[...] + acc_ref[...]


def _merge(oa, ob, oc, gates, b_gate, w_branch, w_out, x, *, tm, tn):
    m, d = x.shape
    tm, tn = _tile(m, tm), _tile(d, tn)
    nj = d // tn
    o_spec = pl.BlockSpec((tm, MIX_W), lambda i, j: (i, 0))
    g_specs = [pl.BlockSpec((tm, tn), functools.partial(lambda i, j, n: (i, n * nj + j), n=n))
               for n in range(N_BRANCH)]
    b_specs = [pl.BlockSpec((1, tn), functools.partial(lambda i, j, n: (0, n * nj + j), n=n))
               for n in range(N_BRANCH)]
    return pl.pallas_call(
        _merge_kernel,
        out_shape=jax.ShapeDtypeStruct((m, d), F32),
        grid=(m // tm, nj),
        in_specs=[o_spec, o_spec, o_spec] + g_specs + b_specs + [
            pl.BlockSpec((N_BRANCH, MIX_W, tn), lambda i, j: (0, 0, j)),
            pl.BlockSpec((tn, d), lambda i, j: (j, 0)),
            pl.BlockSpec((tm, d), lambda i, j: (i, 0)),
        ],
        out_specs=pl.BlockSpec((tm, d), lambda i, j: (i, 0)),
        scratch_shapes=[pltpu.VMEM((tm, d), F32)],
        compiler_params=pltpu.CompilerParams(
            dimension_semantics=("parallel", "arbitrary"),
            vmem_limit_bytes=V7X_VMEM_LIMIT_BYTES),
        name="gated_merge_out_proj",
    )(oa, ob, oc, gates, gates, gates, b_gate, b_gate, b_gate, w_branch, w_out, x)


def _ffn_kernel(*refs, final_norm):
    if final_norm:
        x_ref, g_ref, wg_ref, wu_ref, wd_ref, fg_ref, o_ref, h_ref, acc_ref = refs
    else:
        x_ref, g_ref, wg_ref, wu_ref, wd_ref, o_ref, h_ref, acc_ref = refs
    j = pl.program_id(1)

    @pl.when(j == 0)
    def _():
        h_ref[...] = _rms(x_ref[...], g_ref[...]).astype(BF16)
        acc_ref[...] = jnp.zeros_like(acc_ref)

    h = h_ref[...]
    a = jax.nn.silu(_dot(h, wg_ref[...])) * _dot(h, wu_ref[...])
    acc_ref[...] += _dot(a.astype(BF16), wd_ref[...])

    @pl.when(j == pl.num_programs(1) - 1)
    def _():
        y = x_ref[...] + acc_ref[...]
        if final_norm:
            y = _rms(y, fg_ref[...])
        o_ref[...] = y


def _ffn(x, g, wg, wu, wd, final_g, *, tm, tf):
    m, d = x.shape
    dff = wg.shape[1]
    tm, tf = _tile(m, tm), _tile(dff, tf)
    vec = pl.BlockSpec((1, d), lambda i, j: (0, 0))
    in_specs = [
        pl.BlockSpec((tm, d), lambda i, j: (i, 0)),
        vec,
        pl.BlockSpec((d, tf), lambda i, j: (0, j)),
        pl.BlockSpec((d, tf), lambda i, j: (0, j)),
        pl.BlockSpec((tf, d), lambda i, j: (j, 0)),
    ]
    args = [x, g.reshape(1, d), wg, wu, wd]
    if final_g is not None:
        in_specs.append(vec)
        args.append(final_g.reshape(1, d))
    return pl.pallas_call(
        functools.partial(_ffn_kernel, final_norm=final_g is not None),
        out_shape=jax.ShapeDtypeStruct((m, d), F32),
        grid=(m // tm, dff // tf),
        in_specs=in_specs,
        out_specs=pl.BlockSpec((tm, d), lambda i, j: (i, 0)),
        scratch_shapes=[pltpu.VMEM((tm, d), BF16), pltpu.VMEM((tm, d), F32)],
        compiler_params=pltpu.CompilerParams(
            dimension_semantics=("parallel", "arbitrary"),
            vmem_limit_bytes=V7X_VMEM_LIMIT_BYTES),
        name="swiglu_ffn",
    )(*args)


def kernel(x, norm_mix_g, norm_ffn_g, w_in, b_gate, lam_q1, lam_k1, lam_q2, lam_k2, subln_w,
           w_branch, w_out, w_ffn_gate, w_ffn_up, w_ffn_down, final_norm_g):
    batch, seq, d = x.shape
    depth = w_in.shape[0]
    n_qkv = 3 * N_BRANCH * MIX_W
    assert w_in.shape[2] == n_qkv + N_BRANCH * d
    xf = x.reshape(batch * seq, d)
    for l in range(depth):
        lambda_init = 0.8 - 0.6 * math.exp(-0.3 * l)
        w_in_l = w_in[l].astype(BF16)
        qkv = _norm_matmul(xf, norm_mix_g[l], w_in_l[:, :n_qkv], BF16, tm=1024, tn=1024,
                           name="norm_qkv_proj")
        gates = _norm_matmul(xf, norm_mix_g[l], w_in_l[:, n_qkv:], F32, tm=1024, tn=1024,
                             name="norm_gate_proj")
        oa = _diff_attention(qkv, lam_q1[l], lam_k1[l], lam_q2[l], lam_k2[l], subln_w[l],
                             lambda_init, batch, seq, tq=256, tk=256, hb=4)
        ob = _sb_attention(qkv, batch, seq, tq=256, tk=256, hb=4)
        oc = _moba_attention(qkv, batch, seq, tq=256, tk=256, hb=4)
        xf = _merge(oa, ob, oc, gates, b_gate[l].reshape(1, -1), w_branch[l].astype(BF16),
                    w_out[l].astype(BF16), xf, tm=512, tn=512)
        xf = _ffn(xf, norm_ffn_g[l], w_ffn_gate[l].astype(BF16), w_ffn_up[l].astype(BF16),
                  w_ffn_down[l].astype(BF16), final_norm_g if l == depth - 1 else None,
                  tm=512, tf=512)
    return xf.reshape(batch, seq, d)
```

```python
import functools
import math

import jax
import jax.numpy as jnp
import numpy as np
from jax import lax
from jax.experimental import pallas as pl
from jax.experimental.pallas import tpu as pltpu

F32 = jnp.float32
BF16 = jnp.bfloat16

HEAD_DIM = 128
N_HEADS = 8
MIX_W = N_HEADS * HEAD_DIM
N_BRANCH = 3
MOBA_BLOCK = 256
MOBA_TOPK = 3
RMS_EPS = 1e-6
N_ALIBI = 2 * N_HEADS
_ALIBI_ALL = 2.0 ** (-8.0 * (np.arange(N_ALIBI) + 1) / N_ALIBI)
ALIBI_DIFF = _ALIBI_ALL[0::2].astype(np.float32)
ALIBI_MOBA = _ALIBI_ALL[1::2].astype(np.float32)

LOG2E = 1.4426950408889634
V7X_VMEM_LIMIT_BYTES = 56 * 1024 * 1024
MASKED = -1e30
EXP_UNDERFLOW = -104.0


def _tile(n, target):
    best = None
    for c in range(128, min(n, target) + 1, 128):
        if n % c == 0:
            best = c
    assert best is not None, (n, target)
    return best


def _dot(a, b):
    return jnp.dot(a, b, preferred_element_type=F32)


def _dot_nt(a, b):
    return lax.dot_general(a, b, (((1,), (1,)), ((), ())), preferred_element_type=F32)


def _rms(x, g):
    ms = jnp.mean(x * x, axis=-1, keepdims=True)
    return x * lax.rsqrt(ms + RMS_EPS) * g


def _norm_matmul_kernel(x_ref, g_ref, w_ref, o_ref, h_ref):
    @pl.when(pl.program_id(1) == 0)
    def _():
        h_ref[...] = _rms(x_ref[...], g_ref[...]).astype(BF16)

    o_ref[...] = _dot(h_ref[...], w_ref[...]).astype(o_ref.dtype)


def _norm_matmul(x, g, w, layer, n, *, tm, tn, name):
    m, d = x.shape
    tm, tn = _tile(m, tm), _tile(n, tn)
    return pl.pallas_call(
        _norm_matmul_kernel,
        out_shape=(jax.ShapeDtypeStruct((m, n), BF16), jax.ShapeDtypeStruct((m, d), BF16)),
        grid=(m // tm, n // tn),
        in_specs=[
            pl.BlockSpec((tm, d), lambda i, j: (i, 0)),
            pl.BlockSpec((1, d), lambda i, j: (0, 0)),
            pl.BlockSpec((None, d, tn), lambda i, j: (layer, 0, j)),
        ],
        out_specs=(pl.BlockSpec((tm, tn), lambda i, j: (i, j)),
                   pl.BlockSpec((tm, d), lambda i, j: (i, 0))),
        compiler_params=pltpu.CompilerParams(
            dimension_semantics=("parallel", "arbitrary"),
            vmem_limit_bytes=V7X_VMEM_LIMIT_BYTES),
        name=name,
    )(x, g.reshape(1, d), w)


def _attn_specs(seq, tq, hb, col_q, col_k, col_v):
    nq = seq // tq
    w = hb * HEAD_DIM
    return [
        pl.BlockSpec((tq, w), lambda b, g, i: (b * nq + i, col_q // hb + g)),
        pl.BlockSpec((seq, w), lambda b, g, i: (b, col_k // hb + g)),
        pl.BlockSpec((seq, w), lambda b, g, i: (b, col_v // hb + g)),
    ]


def _attn_out_spec(seq, tq, hb):
    nq = seq // tq
    return pl.BlockSpec((tq, hb * HEAD_DIM), lambda b, g, i: (b * nq + i, g))


_ATTN_PARAMS = pltpu.CompilerParams(
    dimension_semantics=("parallel", "parallel", "arbitrary"),
    vmem_limit_bytes=V7X_VMEM_LIMIT_BYTES)


def _head(ref, s, rows=slice(None)):
    return ref[rows, s * HEAD_DIM:(s + 1) * HEAD_DIM]


def _transpose_v(v_ref, vt_ref, hb):
    for s in range(hb):
        vt_ref[s] = _head(v_ref, s).T


def _key_query_index(tk, n, tq):
    c = lax.broadcasted_iota(jnp.int32, (tk, n), 0)
    r = lax.broadcasted_iota(jnp.int32, (tk, n), 1)
    if n != tq:
        r = r % tq
    return c, r


def _softmax_tiles(i, tq, tk, n, vt_ref, scores, stat_bias=None):
    n_full = i * (tq // tk)
    kidx, qidx = _key_query_index(tk, n, tq)
    heads = range(len(scores))

    def tiles(j, carry, masked):
        k0 = pl.multiple_of(j * tk, tk)
        us = [scores[s](k0) for s in heads]
        ps = []
        for s in heads:
            m, l, acc = carry[s]
            u, off = us[s]
            if masked:
                u = jnp.where(kidx + (k0 - i * tq) <= qidx, u, -jnp.inf)
            if stat_bias is not None:
                off = off + stat_bias[s](k0)
            m_new = jnp.maximum(m, jnp.max(u, axis=0, keepdims=True) + off)
            p = jnp.exp2(u - (m_new - off))
            alpha = jnp.exp2(m - m_new)
            l = alpha * l + jnp.sum(p, axis=0, keepdims=True)
            ps.append((m_new, l, alpha, p.astype(BF16)))
        out = []
        for s in heads:
            m_new, l, alpha, p = ps[s]
            acc = alpha * carry[s][2] + _dot(vt_ref[s, :, pl.ds(k0, tk)], p)
            out.append((m_new, l, acc))
        return tuple(out)

    carry = tuple((jnp.full((1, n), -jnp.inf, F32), jnp.zeros((1, n), F32),
                   jnp.zeros((HEAD_DIM, n), F32)) for _ in heads)
    for d in range(tq // tk):
        carry = tiles(n_full + d, carry, True)
    carry = lax.fori_loop(0, n_full, lambda j, c: tiles(j, c, False), carry)
    return [(acc, l) for _, l, acc in carry]


def _diff_kernel(slopes_ref, lq1_ref, lk1_ref, lq2_ref, lk2_ref, subw_ref,
                 q_ref, k_ref, v_ref, o_ref, vt_ref, *, lambda_init, tq, tk, hb):
    g = pl.program_id(1)
    i = pl.program_id(2)

    @pl.when(i == 0)
    def _():
        _transpose_v(v_ref, vt_ref, hb)

    half = HEAD_DIM // 2
    n = 2 * tq
    key_pos = lax.broadcasted_iota(jnp.int32, (tk, n), 0).astype(F32)
    lane = lax.broadcasted_iota(jnp.int32, (tq, HEAD_DIM), 1)

    def make_scores(s):
        slope2 = slopes_ref[g * hb + s] * LOG2E
        q = _head(q_ref, s) * jnp.asarray(half ** -0.5, BF16)
        zero = jnp.zeros_like(q)
        qq = jnp.concatenate([jnp.where(lane < half, q, zero), jnp.where(lane < half, zero, q)],
                             axis=0)
        key_bias = slope2 * key_pos

        def scores(k0):
            u = _dot_nt(_head(k_ref, s, pl.ds(k0, tk)), qq) * LOG2E + key_bias
            return u, slope2 * (k0 - i * tq).astype(F32)
        return scores

    outs = _softmax_tiles(i, tq, tk, n, vt_ref, [make_scores(s) for s in range(hb)])
    lam = (jnp.exp(jnp.sum(lq1_ref[...] * lk1_ref[...], axis=1, keepdims=True))
           - jnp.exp(jnp.sum(lq2_ref[...] * lk2_ref[...], axis=1, keepdims=True))
           + lambda_init)
    for s, (acc, l) in enumerate(outs):
        o = acc / l
        o = (o[:, :tq] - lam * o[:, tq:]).T
        o = _rms(o, subw_ref[...]) * (1.0 - lambda_init)
        o_ref[:, s * HEAD_DIM:(s + 1) * HEAD_DIM] = o.astype(o_ref.dtype)


def _diff_attention(qkv, lq1, lk1, lq2, lk2, subw, lambda_init, batch, seq, *, tq, tk, hb):
    half = HEAD_DIM // 2
    small = pl.BlockSpec((1, half), lambda b, g, i: (0, 0))
    return pl.pallas_call(
        functools.partial(_diff_kernel, lambda_init=lambda_init, tq=tq, tk=tk, hb=hb),
        out_shape=jax.ShapeDtypeStruct((batch * seq, MIX_W), BF16),
        grid=(batch, N_HEADS // hb, seq // tq),
        in_specs=[pl.BlockSpec(memory_space=pltpu.SMEM), small, small, small, small,
                  pl.BlockSpec((1, HEAD_DIM), lambda b, g, i: (0, 0))]
                 + _attn_specs(seq, tq, hb, 0, N_HEADS, 2 * N_HEADS),
        out_specs=_attn_out_spec(seq, tq, hb),
        scratch_shapes=[pltpu.VMEM((hb, HEAD_DIM, seq), BF16)],
        compiler_params=_ATTN_PARAMS,
        name="diff_attention",
    )(jnp.asarray(ALIBI_DIFF), lq1.reshape(1, half), lk1.reshape(1, half),
      lq2.reshape(1, half), lk2.reshape(1, half), subw.reshape(1, HEAD_DIM), qkv, qkv, qkv)


def _sb_kernel(q_ref, k_ref, v_ref, o_ref, vt_ref, *, tq, tk, hb):
    i = pl.program_id(2)

    @pl.when(i == 0)
    def _():
        _transpose_v(v_ref, vt_ref, hb)

    scale = HEAD_DIM ** -0.5
    n_full = i * (tq // tk)
    kidx, qidx = _key_query_index(tk, tq, tq)
    a = lax.broadcasted_iota(jnp.int32, (tk, tk), 0)
    b = lax.broadcasted_iota(jnp.int32, (tk, tk), 1)
    later = (b > a).astype(BF16)
    qs = [_head(q_ref, s) for s in range(hb)]
    heads = range(hb)

    def tiles(j, carry, masked):
        k0 = pl.multiple_of(j * tk, tk)
        zs = [_dot_nt(_head(k_ref, s, pl.ds(k0, tk)), qs[s]) * scale for s in heads]
        if masked:
            strict = kidx + (k0 - i * tq) < qidx
        mid = []
        for s in heads:
            z = zs[s]
            sp = jnp.log1p(jnp.exp(-jnp.abs(z)))
            log_beta = jnp.minimum(z, 0.0) - sp
            log_fail = -jnp.maximum(z, 0.0) - sp
            if masked:
                log_fail = jnp.where(strict, log_fail, 0.0)
            hi = log_fail.astype(BF16)
            lo = (log_fail - hi.astype(F32)).astype(BF16)
            mid.append((log_beta, log_fail, hi, lo))
        sufs = [_dot(later, mid[s][2]) + _dot(later, mid[s][3]) for s in heads]
        ws = []
        for s in heads:
            w = jnp.exp(mid[s][0] + sufs[s] + carry[s][1])
            if masked:
                w = jnp.where(strict, w, 0.0)
            ws.append(w.astype(BF16))
        out = []
        for s in heads:
            acc = carry[s][0] + _dot(vt_ref[s, :, pl.ds(k0, tk)], ws[s])
            run = carry[s][1] + jnp.sum(mid[s][1], axis=0, keepdims=True)
            out.append((acc, run))
        return tuple(out)

    def alive(carry):
        top = carry[0][1]
        for s in heads[1:]:
            top = jnp.maximum(top, carry[s][1])
        return jnp.max(top) >= EXP_UNDERFLOW

    carry = tuple((jnp.zeros((HEAD_DIM, tq), F32), jnp.zeros((1, tq), F32)) for _ in heads)
    for d in reversed(range(tq // tk)):
        carry = tiles(n_full + d, carry, True)

    def body(state):
        t, _, c = state
        c = tiles(n_full - 1 - t, c, False)
        return t + 1, alive(c), c

    _, _, carry = lax.while_loop(lambda st: jnp.logical_and(st[0] < n_full, st[1]), body,
                                 (jnp.int32(0), alive(carry), carry))
    for s in heads:
        o_ref[:, s * HEAD_DIM:(s + 1) * HEAD_DIM] = carry[s][0].T.astype(o_ref.dtype)


def _sb_attention(qkv, batch, seq, *, tq, tk, hb):
    return pl.pallas_call(
        functools.partial(_sb_kernel, tq=tq, tk=tk, hb=hb),
        out_shape=jax.ShapeDtypeStruct((batch * seq, MIX_W), BF16),
        grid=(batch, N_HEADS // hb, seq // tq),
        in_specs=_attn_specs(seq, tq, hb, 3 * N_HEADS, 4 * N_HEADS, 5 * N_HEADS),
        out_specs=_attn_out_spec(seq, tq, hb),
        scratch_shapes=[pltpu.VMEM((hb, HEAD_DIM, seq), BF16)],
        compiler_params=_ATTN_PARAMS,
        name="stick_breaking_attention",
    )(qkv, qkv, qkv)


def _moba_kernel(slopes_ref, q_ref, k_ref, v_ref, o_ref, vt_ref, kmean_ref, selb_ref,
                 *, tq, tk, hb):
    nb = k_ref.shape[0] // MOBA_BLOCK
    g = pl.program_id(1)
    i = pl.program_id(2)
    own = (i * tq) // MOBA_BLOCK

    @pl.when(i == 0)
    def _():
        _transpose_v(v_ref, vt_ref, hb)
        for s in range(hb):
            kf = _head(k_ref, s).astype(F32).reshape(nb, MOBA_BLOCK, HEAD_DIM)
            kmean_ref[s] = jnp.mean(kf, axis=1)

    c1 = HEAD_DIM ** -0.5 * LOG2E
    n_idx = lax.broadcasted_iota(jnp.int32, (nb, tq), 0)
    key_pos = lax.broadcasted_iota(jnp.int32, (tk, tq), 0).astype(F32)

    def make_head(s):
        slope2 = slopes_ref[g * hb + s] * LOG2E
        q = _head(q_ref, s)
        km = kmean_ref[s]
        km_hi = km.astype(BF16)
        km_lo = (km - km_hi.astype(F32)).astype(BF16)
        gate = _dot_nt(km_hi, q) + _dot_nt(km_lo, q)
        beaten = jnp.zeros((nb, tq), jnp.int32)
        for mth in range(nb):
            gm = gate[mth:mth + 1, :]
            beats = (gm > gate) | ((gm == gate) & (mth < n_idx))
            beaten = beaten + jnp.where(beats & (mth < own), 1, 0)
        visible = (n_idx == own) | ((n_idx < own) & (beaten < MOBA_TOPK))
        selb_ref[s] = jnp.where(visible, 0.0, MASKED)
        key_bias = slope2 * key_pos

        def scores(k0):
            u = _dot_nt(_head(k_ref, s, pl.ds(k0, tk)), q) * c1 + key_bias
            return u, slope2 * (k0 - i * tq).astype(F32)

        def block_bias(k0):
            return selb_ref[s, pl.ds(k0 // MOBA_BLOCK, 1), :]
        return scores, block_bias

    fns = [make_head(s) for s in range(hb)]
    outs = _softmax_tiles(i, tq, tk, tq, vt_ref, [f[0] for f in fns], [f[1] for f in fns])
    for s, (acc, l) in enumerate(outs):
        o_ref[:, s * HEAD_DIM:(s + 1) * HEAD_DIM] = (acc / l).T.astype(o_ref.dtype)


def _moba_attention(qkv, batch, seq, *, tq, tk, hb):
    assert seq % MOBA_BLOCK == 0 and MOBA_BLOCK % tk == 0 and MOBA_BLOCK % tq == 0
    nb = seq // MOBA_BLOCK
    return pl.pallas_call(
        functools.partial(_moba_kernel, tq=tq, tk=tk, hb=hb),
        out_shape=jax.ShapeDtypeStruct((batch * seq, MIX_W), BF16),
        grid=(batch, N_HEADS // hb, seq // tq),
        in_specs=[pl.BlockSpec(memory_space=pltpu.SMEM)]
                 + _attn_specs(seq, tq, hb, 6 * N_HEADS, 7 * N_HEADS, 8 * N_HEADS),
        out_specs=_attn_out_spec(seq, tq, hb),
        scratch_shapes=[pltpu.VMEM((hb, HEAD_DIM, seq), BF16),
                        pltpu.VMEM((hb, nb, HEAD_DIM), F32),
                        pltpu.VMEM((hb, nb, tq), F32)],
        compiler_params=_ATTN_PARAMS,
        name="moba_attention",
    )(jnp.asarray(ALIBI_MOBA), qkv, qkv, qkv)


def _gate_merge_kernel(h_ref, oa_ref, ob_ref, oc_ref, wga_ref, wgb_ref, wgc_ref,
                       ba_ref, bb_ref, bc_ref, wbr_ref, o_ref):
    h = h_ref[...]
    merged = None
    for n, (o_r, wg_r, b_r) in enumerate(((oa_ref, wga_ref, ba_ref), (ob_ref, wgb_ref, bb_ref),
                                          (oc_ref, wgc_ref, bc_ref))):
        gate = jax.nn.sigmoid(_dot(h, wg_r[...]) + b_r[...])
        term = gate * _dot(o_r[...], wbr_ref[n])
        merged = term if merged is None else merged + term
    o_ref[...] = merged.astype(o_ref.dtype)


def _gate_merge(h, oa, ob, oc, w_in, b_gate, w_branch, layer, gate_col0, *, tm, tn):
    m, d = h.shape
    tm, tn = _tile(m, tm), _tile(d, tn)
    nj = d // tn
    assert gate_col0 % tn == 0
    c0 = gate_col0 // tn
    o_spec = pl.BlockSpec((tm, MIX_W), lambda i, j: (i, 0))
    wg_specs = [pl.BlockSpec((None, d, tn),
                             functools.partial(lambda i, j, n: (layer, 0, c0 + n * nj + j), n=n))
                for n in range(N_BRANCH)]
    b_specs = [pl.BlockSpec((None, 1, tn),
                            functools.partial(lambda i, j, n: (layer, 0, n * nj + j), n=n))
               for n in range(N_BRANCH)]
    b_gate = b_gate.reshape(b_gate.shape[0], 1, -1)
    return pl.pallas_call(
        _gate_merge_kernel,
        out_shape=jax.ShapeDtypeStruct((m, d), BF16),
        grid=(m // tm, nj),
        in_specs=[pl.BlockSpec((tm, d), lambda i, j: (i, 0)), o_spec, o_spec, o_spec]
                 + wg_specs + b_specs
                 + [pl.BlockSpec((None, N_BRANCH, MIX_W, tn), lambda i, j: (layer, 0, 0, j))],
        out_specs=pl.BlockSpec((tm, tn), lambda i, j: (i, j)),
        compiler_params=pltpu.CompilerParams(
            dimension_semantics=("parallel", "arbitrary"),
            vmem_limit_bytes=V7X_VMEM_LIMIT_BYTES),
        name="gate_merge",
    )(h, oa, ob, oc, w_in, w_in, w_in, b_gate, b_gate, b_gate, w_branch)


def _out_proj_kernel(a_ref, w_ref, x_ref, o_ref):
    o_ref[...] = x_ref[...] + _dot(a_ref[...], w_ref[...])


def _out_proj(a, w, layer, x, *, tm, tn):
    m, d = x.shape
    k = a.shape[1]
    tm, tn = _tile(m, tm), _tile(d, tn)
    return pl.pallas_call(
        _out_proj_kernel,
        out_shape=jax.ShapeDtypeStruct((m, d), F32),
        grid=(m // tm, d // tn),
        in_specs=[pl.BlockSpec((tm, k), lambda i, j: (i, 0)),
                  pl.BlockSpec((None, k, tn), lambda i, j: (layer, 0, j)),
                  pl.BlockSpec((tm, tn), lambda i, j: (i, j))],
        out_specs=pl.BlockSpec((tm, tn), lambda i, j: (i, j)),
        compiler_params=pltpu.CompilerParams(
            dimension_semantics=("parallel", "arbitrary"),
            vmem_limit_bytes=V7X_VMEM_LIMIT_BYTES),
        name="out_proj_residual",
    )(a, w, x)


def _ffn_kernel(*refs, final_norm):
    if final_norm:
        x_ref, g_ref, wg_ref, wu_ref, wd_ref, fg_ref, o_ref, h_ref, acc_ref = refs
    else:
        x_ref, g_ref, wg_ref, wu_ref, wd_ref, o_ref, h_ref, acc_ref = refs
    j = pl.program_id(1)

    @pl.when(j == 0)
    def _():
        h_ref[...] = _rms(x_ref[...], g_ref[...]).astype(BF16)
        acc_ref[...] = jnp.zeros_like(acc_ref)

    h = h_ref[...]
    a = jax.nn.silu(_dot(h, wg_ref[...])) * _dot(h, wu_ref[...])
    acc_ref[...] += _dot(a.astype(BF16), wd_ref[...])

    @pl.when(j == pl.num_programs(1) - 1)
    def _():
        y = x_ref[...] + acc_ref[...]
        if final_norm:
            y = _rms(y, fg_ref[...])
        o_ref[...] = y


def _ffn(x, g, wg, wu, wd, layer, final_g, *, tm, tf):
    m, d = x.shape
    dff = wg.shape[2]
    tm, tf = _tile(m, tm), _tile(dff, tf)
    vec = pl.BlockSpec((1, d), lambda i, j: (0, 0))
    in_specs = [
        pl.BlockSpec((tm, d), lambda i, j: (i, 0)),
        vec,
        pl.BlockSpec((None, d, tf), lambda i, j: (layer, 0, j)),
        pl.BlockSpec((None, d, tf), lambda i, j: (layer, 0, j)),
        pl.BlockSpec((None, tf, d), lambda i, j: (layer, j, 0)),
    ]
    args = [x, g.reshape(1, d), wg, wu, wd]
    if final_g is not None:
        in_specs.append(vec)
        args.append(final_g.reshape(1, d))
    return pl.pallas_call(
        functools.partial(_ffn_kernel, final_norm=final_g is not None),
        out_shape=jax.ShapeDtypeStruct((m, d), F32),
        grid=(m // tm, dff // tf),
        in_specs=in_specs,
        out_specs=pl.BlockSpec((tm, d), lambda i, j: (i, 0)),
        scratch_shapes=[pltpu.VMEM((tm, d), BF16), pltpu.VMEM((tm, d), F32)],
        compiler_params=pltpu.CompilerParams(
            dimension_semantics=("parallel", "arbitrary"),
            vmem_limit_bytes=V7X_VMEM_LIMIT_BYTES),
        name="swiglu_ffn",
    )(*args)


def kernel(x, norm_mix_g, norm_ffn_g, w_in, b_gate, lam_q1, lam_k1, lam_q2, lam_k2, subln_w,
           w_branch, w_out, w_ffn_gate, w_ffn_up, w_ffn_down, final_norm_g):
    batch, seq, d = x.shape
    depth = w_in.shape[0]
    n_qkv = 3 * N_BRANCH * MIX_W
    assert w_in.shape[2] == n_qkv + N_BRANCH * d
    xf = x.reshape(batch * seq, d)
    w_in, w_branch, w_out, w_ffn_gate, w_ffn_up, w_ffn_down = (
        w.astype(BF16) for w in (w_in, w_branch, w_out, w_ffn_gate, w_ffn_up, w_ffn_down))
    for l in range(depth):
        lambda_init = 0.8 - 0.6 * math.exp(-0.3 * l)
        qkv, h = _norm_matmul(xf, norm_mix_g[l], w_in, l, n_qkv, tm=1024, tn=1024,
                              name="norm_qkv_proj")
        oa = _diff_attention(qkv, lam_q1[l], lam_k1[l], lam_q2[l], lam_k2[l], subln_w[l],
                             lambda_init, batch, seq, tq=256, tk=256, hb=8)
        ob = _sb_attention(qkv, batch, seq, tq=256, tk=256, hb=8)
        oc = _moba_attention(qkv, batch, seq, tq=256, tk=256, hb=8)
        merged = _gate_merge(h, oa, ob, oc, w_in, b_gate, w_branch, l, n_qkv, tm=1024, tn=512)
        xf = _out_proj(merged, w_out, l, xf, tm=1024, tn=1024)
        xf = _ffn(xf, norm_ffn_g[l], w_ffn_gate, w_ffn_up, w_ffn_down, l,
                  final_norm_g if l == depth - 1 else None, tm=512, tf=512)
    return xf.reshape(batch, seq, d)
```

```python
import functools
import math

import jax
import jax.numpy as jnp
import numpy as np
from jax import lax
from jax.experimental import pallas as pl
from jax.experimental.pallas import tpu as pltpu

F32 = jnp.float32
BF16 = jnp.bfloat16

HEAD_DIM = 128
N_HEADS = 8
MIX_W = N_HEADS * HEAD_DIM
N_BRANCH = 3
MOBA_BLOCK = 256
MOBA_TOPK = 3
RMS_EPS = 1e-6
N_ALIBI = 2 * N_HEADS
_ALIBI_ALL = 2.0 ** (-8.0 * (np.arange(N_ALIBI) + 1) / N_ALIBI)
ALIBI_DIFF = _ALIBI_ALL[0::2].astype(np.float32)
ALIBI_MOBA = _ALIBI_ALL[1::2].astype(np.float32)

LOG2E = 1.4426950408889634
V7X_VMEM_LIMIT_BYTES = 56 * 1024 * 1024
MASKED = -1e30
VT_ROWS = HEAD_DIM + 16
EXP2_UNDERFLOW = -151.0


def _tile(n, target):
    best = None
    for c in range(128, min(n, target) + 1, 128):
        if n % c == 0:
            best = c
    assert best is not None, (n, target)
    return best


def _dot(a, b):
    return jnp.dot(a, b, preferred_element_type=F32)


def _dot_nt(a, b):
    return lax.dot_general(a, b, (((1,), (1,)), ((), ())), preferred_element_type=F32)


def _rms(x, g):
    ms = jnp.mean(x * x, axis=-1, keepdims=True)
    return x * lax.rsqrt(ms + RMS_EPS) * g


def _norm_matmul_kernel(x_ref, g_ref, w_ref, o_ref, h_ref):
    @pl.when(pl.program_id(1) == 0)
    def _():
        h_ref[...] = _rms(x_ref[...], g_ref[...]).astype(BF16)

    o_ref[...] = _dot(h_ref[...], w_ref[...]).astype(o_ref.dtype)


def _norm_matmul(x, g, w, layer, n, *, tm, tn, name):
    m, d = x.shape
    tm, tn = _tile(m, tm), _tile(n, tn)
    return pl.pallas_call(
        _norm_matmul_kernel,
        out_shape=(jax.ShapeDtypeStruct((m, n), BF16), jax.ShapeDtypeStruct((m, d), BF16)),
        grid=(m // tm, n // tn),
        in_specs=[
            pl.BlockSpec((tm, d), lambda i, j: (i, 0)),
            pl.BlockSpec((1, d), lambda i, j: (0, 0)),
            pl.BlockSpec((None, d, tn), lambda i, j: (layer, 0, j)),
        ],
        out_specs=(pl.BlockSpec((tm, tn), lambda i, j: (i, j)),
                   pl.BlockSpec((tm, d), lambda i, j: (i, 0))),
        compiler_params=pltpu.CompilerParams(
            dimension_semantics=("parallel", "arbitrary"),
            vmem_limit_bytes=V7X_VMEM_LIMIT_BYTES),
        name=name,
    )(x, g.reshape(1, d), w)


def _attn_specs(seq, tq, hb, col_q, col_k, col_v):
    nq = seq // tq
    w = hb * HEAD_DIM
    return [
        pl.BlockSpec((tq, w), lambda b, g, i: (b * nq + i, col_q // hb + g)),
        pl.BlockSpec((seq, w), lambda b, g, i: (b, col_k // hb + g)),
        pl.BlockSpec((seq, w), lambda b, g, i: (b, col_v // hb + g)),
    ]


def _attn_out_spec(seq, tq, hb):
    nq = seq // tq
    return pl.BlockSpec((tq, hb * HEAD_DIM), lambda b, g, i: (b * nq + i, g))


_ATTN_PARAMS = pltpu.CompilerParams(
    dimension_semantics=("parallel", "parallel", "arbitrary"),
    vmem_limit_bytes=V7X_VMEM_LIMIT_BYTES)


def _head(ref, s, rows=slice(None)):
    return ref[rows, s * HEAD_DIM:(s + 1) * HEAD_DIM]


def _transpose_v(v_ref, vt_ref, hb, ones_rows=False):
    for s in range(hb):
        vt_ref[s, :HEAD_DIM, :] = _head(v_ref, s).T
        if ones_rows:
            vt_ref[s, HEAD_DIM:, :] = jnp.ones((VT_ROWS - HEAD_DIM, vt_ref.shape[2]), vt_ref.dtype)


def _bias_columns(rows, slope):
    lane = lax.broadcasted_iota(jnp.int32, (rows, HEAD_DIM), 1)
    x = jnp.full((rows, HEAD_DIM), slope, F32)
    hi = x.astype(BF16).astype(F32)
    mid = (x - hi).astype(BF16).astype(F32)
    lo = x - hi - mid
    cols = jnp.where(lane == 0, hi, jnp.where(lane == 1, mid, jnp.where(lane == 2, lo, 0.0)))
    return cols.astype(BF16)


def _position_columns(tk):
    lane = lax.broadcasted_iota(jnp.int32, (tk, HEAD_DIM), 1)
    c = lax.broadcasted_iota(jnp.int32, (tk, HEAD_DIM), 0)
    return jnp.where(lane < 3, c, 0).astype(F32).astype(BF16)


def _key_query_index(tk, n, tq):
    c = lax.broadcasted_iota(jnp.int32, (tk, n), 0)
    r = lax.broadcasted_iota(jnp.int32, (tk, n), 1)
    if n != tq:
        r = r % tq
    return c, r


def _softmax_tiles(i, tq, tk, n, vt_ref, scores, stat_bias=None):
    n_full = i * (tq // tk)
    kidx, qidx = _key_query_index(tk, n, tq)
    heads = range(len(scores))

    def tiles(j, carry, masked):
        k0 = pl.multiple_of(j * tk, tk)
        us = [scores[s](k0) for s in heads]
        ps = []
        for s in heads:
            m, acc = carry[s]
            u, off = us[s]
            if masked:
                u = jnp.where(kidx + (k0 - i * tq) <= qidx, u, -jnp.inf)
            if stat_bias is not None:
                off = off + stat_bias[s](k0)
            m_new = jnp.maximum(m, jnp.max(u, axis=0, keepdims=True) + off)
            p = jnp.exp2(u - (m_new - off))
            ps.append((m_new, jnp.exp2(m - m_new), p.astype(BF16)))
        out = []
        for s in heads:
            m_new, alpha, p = ps[s]
            acc = alpha * carry[s][1] + _dot(vt_ref[s, :, pl.ds(k0, tk)], p)
            out.append((m_new, acc))
        return tuple(out)

    carry = tuple((jnp.full((1, n), -jnp.inf, F32), jnp.zeros((VT_ROWS, n), F32)) for _ in heads)
    for d in range(tq // tk):
        carry = tiles(n_full + d, carry, True)
    carry = lax.fori_loop(0, n_full, lambda j, c: tiles(j, c, False), carry)
    return [(acc[:HEAD_DIM], acc[HEAD_DIM:HEAD_DIM + 1]) for _, acc in carry]


def _diff_kernel(slopes_ref, lq1_ref, lk1_ref, lq2_ref, lk2_ref, subw_ref,
                 q_ref, k_ref, v_ref, o_ref, vt_ref, *, lambda_init, tq, tk, hb):
    g = pl.program_id(1)
    i = pl.program_id(2)

    @pl.when(i == 0)
    def _():
        _transpose_v(v_ref, vt_ref, hb, ones_rows=True)

    half = HEAD_DIM // 2
    n = 2 * tq
    pos_cols = _position_columns(tk)
    lane = lax.broadcasted_iota(jnp.int32, (tq, HEAD_DIM), 1)

    def make_scores(s):
        slope2 = slopes_ref[g * hb + s] * LOG2E
        q = _head(q_ref, s) * jnp.asarray(half ** -0.5, BF16)
        zero = jnp.zeros_like(q)
        qq = jnp.concatenate([jnp.where(lane < half, q, zero), jnp.where(lane < half, zero, q)],
                             axis=0)
        qq = jnp.concatenate([qq, _bias_columns(n, slopes_ref[g * hb + s])], axis=1)

        def scores(k0):
            ka = jnp.concatenate([_head(k_ref, s, pl.ds(k0, tk)), pos_cols], axis=1)
            return _dot_nt(ka, qq) * LOG2E, slope2 * (k0 - i * tq).astype(F32)
        return scores

    outs = _softmax_tiles(i, tq, tk, n, vt_ref, [make_scores(s) for s in range(hb)])
    lam = (jnp.exp(jnp.sum(lq1_ref[...] * lk1_ref[...], axis=1, keepdims=True))
           - jnp.exp(jnp.sum(lq2_ref[...] * lk2_ref[...], axis=1, keepdims=True))
           + lambda_init)
    for s, (acc, l) in enumerate(outs):
        o = acc / l
        o = (o[:, :tq] - lam * o[:, tq:]).T
        o = _rms(o, subw_ref[...]) * (1.0 - lambda_init)
        o_ref[:, s * HEAD_DIM:(s + 1) * HEAD_DIM] = o.astype(o_ref.dtype)


def _diff_attention(qkv, lq1, lk1, lq2, lk2, subw, lambda_init, batch, seq, *, tq, tk, hb):
    half = HEAD_DIM // 2
    small = pl.BlockSpec((1, half), lambda b, g, i: (0, 0))
    return pl.pallas_call(
        functools.partial(_diff_kernel, lambda_init=lambda_init, tq=tq, tk=tk, hb=hb),
        out_shape=jax.ShapeDtypeStruct((batch * seq, MIX_W), BF16),
        grid=(batch, N_HEADS // hb, seq // tq),
        in_specs=[pl.BlockSpec(memory_space=pltpu.SMEM), small, small, small, small,
                  pl.BlockSpec((1, HEAD_DIM), lambda b, g, i: (0, 0))]
                 + _attn_specs(seq, tq, hb, 0, N_HEADS, 2 * N_HEADS),
        out_specs=_attn_out_spec(seq, tq, hb),
        scratch_shapes=[pltpu.VMEM((hb, VT_ROWS, seq), BF16)],
        compiler_params=_ATTN_PARAMS,
        name="diff_attention",
    )(jnp.asarray(ALIBI_DIFF), lq1.reshape(1, half), lk1.reshape(1, half),
      lq2.reshape(1, half), lk2.reshape(1, half), subw.reshape(1, HEAD_DIM), qkv, qkv, qkv)


def _sb_kernel(q_ref, k_ref, v_ref, o_ref, vt_ref, *, tq, tk, hb):
    i = pl.program_id(2)

    @pl.when(i == 0)
    def _():
        _transpose_v(v_ref, vt_ref, hb)

    scale2 = HEAD_DIM ** -0.5 * LOG2E
    n_full = i * (tq // tk)
    kidx, qidx = _key_query_index(tk, tq, tq)
    a = lax.broadcasted_iota(jnp.int32, (tk + 16, tk), 0)
    b = lax.broadcasted_iota(jnp.int32, (tk + 16, tk), 1)
    later = jnp.where((b > a) | (a >= tk), 1.0, 0.0).astype(BF16)
    qs = [_head(q_ref, s) for s in range(hb)]
    heads = range(hb)

    def tiles(j, carry, masked):
        k0 = pl.multiple_of(j * tk, tk)
        zs = [_dot_nt(_head(k_ref, s, pl.ds(k0, tk)), qs[s]) * scale2 for s in heads]
        if masked:
            strict = kidx + (k0 - i * tq) < qidx
        mid = []
        for s in heads:
            z = zs[s]
            sp = jnp.log2(1.0 + jnp.exp2(-jnp.abs(z)))
            log_beta = jnp.minimum(z, 0.0) - sp
            log_fail = -jnp.maximum(z, 0.0) - sp
            if masked:
                log_fail = jnp.where(strict, log_fail, 0.0)
            hi = log_fail.astype(BF16)
            lo = (log_fail - hi.astype(F32)).astype(BF16)
            mid.append((log_beta, hi, lo))
        sufs = [_dot(later, mid[s][1]) + _dot(later, mid[s][2]) for s in heads]
        ws = []
        for s in heads:
            w = jnp.exp2(mid[s][0] + sufs[s][:tk] + carry[s][1])
            if masked:
                w = jnp.where(strict, w, 0.0)
            ws.append(w.astype(BF16))
        out = []
        for s in heads:
            acc = carry[s][0] + _dot(vt_ref[s, :, pl.ds(k0, tk)], ws[s])
            run = carry[s][1] + sufs[s][tk:tk + 1]
            out.append((acc, run))
        return tuple(out)

    def alive(carry):
        top = carry[0][1]
        for s in heads[1:]:
            top = jnp.maximum(top, carry[s][1])
        return jnp.max(top) >= EXP2_UNDERFLOW

    carry = tuple((jnp.zeros((HEAD_DIM, tq), F32), jnp.zeros((1, tq), F32)) for _ in heads)
    for d in reversed(range(tq // tk)):
        carry = tiles(n_full + d, carry, True)

    def body(state):
        t, _, c = state
        c = tiles(n_full - 1 - t, c, False)
        return t + 1, alive(c), c

    _, _, carry = lax.while_loop(lambda st: jnp.logical_and(st[0] < n_full, st[1]), body,
                                 (jnp.int32(0), alive(carry), carry))
    for s in heads:
        o_ref[:, s * HEAD_DIM:(s + 1) * HEAD_DIM] = carry[s][0].T.astype(o_ref.dtype)


def _sb_attention(qkv, batch, seq, *, tq, tk, hb):
    return pl.pallas_call(
        functools.partial(_sb_kernel, tq=tq, tk=tk, hb=hb),
        out_shape=jax.ShapeDtypeStruct((batch * seq, MIX_W), BF16),
        grid=(batch, N_HEADS // hb, seq // tq),
        in_specs=_attn_specs(seq, tq, hb, 3 * N_HEADS, 4 * N_HEADS, 5 * N_HEADS),
        out_specs=_attn_out_spec(seq, tq, hb),
        scratch_shapes=[pltpu.VMEM((hb, HEAD_DIM, seq), BF16)],
        compiler_params=_ATTN_PARAMS,
        name="stick_breaking_attention",
    )(qkv, qkv, qkv)


def _moba_kernel(slopes_ref, q_ref, k_ref, v_ref, o_ref, vt_ref, kmean_ref, selb_ref,
                 *, tq, tk, hb):
    nb = k_ref.shape[0] // MOBA_BLOCK
    g = pl.program_id(1)
    i = pl.program_id(2)
    own = (i * tq) // MOBA_BLOCK

    @pl.when(i == 0)
    def _():
        _transpose_v(v_ref, vt_ref, hb, ones_rows=True)
        for s in range(hb):
            kf = _head(k_ref, s).astype(F32).reshape(nb, MOBA_BLOCK, HEAD_DIM)
            kmean_ref[s] = jnp.mean(kf, axis=1)

    c1 = HEAD_DIM ** -0.5 * LOG2E
    n_idx = lax.broadcasted_iota(jnp.int32, (nb, tq), 0)
    pos_cols = _position_columns(tk)

    def make_head(s):
        slope2 = slopes_ref[g * hb + s] * LOG2E
        q = _head(q_ref, s)
        km = kmean_ref[s]
        km_hi = km.astype(BF16)
        km_lo = (km - km_hi.astype(F32)).astype(BF16)
        gate = _dot_nt(km_hi, q) + _dot_nt(km_lo, q)
        beaten = jnp.zeros((nb, tq), jnp.int32)
        for mth in range(nb):
            gm = gate[mth:mth + 1, :]
            beats = (gm > gate) | ((gm == gate) & (mth < n_idx))
            beaten = beaten + jnp.where(beats & (mth < own), 1, 0)
        visible = (n_idx == own) | ((n_idx < own) & (beaten < MOBA_TOPK))
        selb_ref[s] = jnp.where(visible, 0.0, MASKED)
        qa = jnp.concatenate([q, _bias_columns(tq, slopes_ref[g * hb + s] * HEAD_DIM ** 0.5)], axis=1)

        def scores(k0):
            ka = jnp.concatenate([_head(k_ref, s, pl.ds(k0, tk)), pos_cols], axis=1)
            return _dot_nt(ka, qa) * c1, slope2 * (k0 - i * tq).astype(F32)

        def block_bias(k0):
            return selb_ref[s, pl.ds(k0 // MOBA_BLOCK, 1), :]
        return scores, block_bias

    fns = [make_head(s) for s in range(hb)]
    outs = _softmax_tiles(i, tq, tk, tq, vt_ref, [f[0] for f in fns], [f[1] for f in fns])
    for s, (acc, l) in enumerate(outs):
        o_ref[:, s * HEAD_DIM:(s + 1) * HEAD_DIM] = (acc / l).T.astype(o_ref.dtype)


def _moba_attention(qkv, batch, seq, *, tq, tk, hb):
    assert seq % MOBA_BLOCK == 0 and MOBA_BLOCK % tk == 0 and MOBA_BLOCK % tq == 0
    nb = seq // MOBA_BLOCK
    return pl.pallas_call(
        functools.partial(_moba_kernel, tq=tq, tk=tk, hb=hb),
        out_shape=jax.ShapeDtypeStruct((batch * seq, MIX_W), BF16),
        grid=(batch, N_HEADS // hb, seq // tq),
        in_specs=[pl.BlockSpec(memory_space=pltpu.SMEM)]
                 + _attn_specs(seq, tq, hb, 6 * N_HEADS, 7 * N_HEADS, 8 * N_HEADS),
        out_specs=_attn_out_spec(seq, tq, hb),
        scratch_shapes=[pltpu.VMEM((hb, VT_ROWS, seq), BF16),
                        pltpu.VMEM((hb, nb, HEAD_DIM), F32),
                        pltpu.VMEM((hb, nb, tq), F32)],
        compiler_params=_ATTN_PARAMS,
        name="moba_attention",
    )(jnp.asarray(ALIBI_MOBA), qkv, qkv, qkv)


def _gate_merge_kernel(h_ref, oa_ref, ob_ref, oc_ref, wga_ref, wgb_ref, wgc_ref,
                       ba_ref, bb_ref, bc_ref, wbr_ref, o_ref):
    h = h_ref[...]
    merged = None
    for n, (o_r, wg_r, b_r) in enumerate(((oa_ref, wga_ref, ba_ref), (ob_ref, wgb_ref, bb_ref),
                                          (oc_ref, wgc_ref, bc_ref))):
        gate = jax.nn.sigmoid(_dot(h, wg_r[...]) + b_r[...])
        term = gate * _dot(o_r[...], wbr_ref[n])
        merged = term if merged is None else merged + term
    o_ref[...] = merged.astype(o_ref.dtype)


def _gate_merge(h, oa, ob, oc, w_in, b_gate, w_branch, layer, gate_col0, *, tm, tn):
    m, d = h.shape
    tm, tn = _tile(m, tm), _tile(d, tn)
    nj = d // tn
    assert gate_col0 % tn == 0
    c0 = gate_col0 // tn
    o_spec = pl.BlockSpec((tm, MIX_W), lambda i, j: (i, 0))
    wg_specs = [pl.BlockSpec((None, d, tn),
                             functools.partial(lambda i, j, n: (layer, 0, c0 + n * nj + j), n=n))
                for n in range(N_BRANCH)]
    b_specs = [pl.BlockSpec((None, 1, tn),
                            functools.partial(lambda i, j, n: (layer, 0, n * nj + j), n=n))
               for n in range(N_BRANCH)]
    b_gate = b_gate.reshape(b_gate.shape[0], 1, -1)
    return pl.pallas_call(
        _gate_merge_kernel,
        out_shape=jax.ShapeDtypeStruct((m, d), BF16),
        grid=(m // tm, nj),
        in_specs=[pl.BlockSpec((tm, d), lambda i, j: (i, 0)), o_spec, o_spec, o_spec]
                 + wg_specs + b_specs
                 + [pl.BlockSpec((None, N_BRANCH, MIX_W, tn), lambda i, j: (layer, 0, 0, j))],
        out_specs=pl.BlockSpec((tm, tn), lambda i, j: (i, j)),
        compiler_params=pltpu.CompilerParams(
            dimension_semantics=("parallel", "arbitrary"),
            vmem_limit_bytes=V7X_VMEM_LIMIT_BYTES),
        name="gate_merge",
    )(h, oa, ob, oc, w_in, w_in, w_in, b_gate, b_gate, b_gate, w_branch)


def _out_proj_kernel(a_ref, w_ref, x_ref, o_ref):
    o_ref[...] = x_ref[...] + _dot(a_ref[...], w_ref[...])


def _out_proj(a, w, layer, x, *, tm, tn):
    m, d = x.shape
    k = a.shape[1]
    tm, tn = _tile(m, tm), _tile(d, tn)
    return pl.pallas_call(
        _out_proj_kernel,
        out_shape=jax.ShapeDtypeStruct((m, d), F32),
        grid=(m // tm, d // tn),
        in_specs=[pl.BlockSpec((tm, k), lambda i, j: (i, 0)),
                  pl.BlockSpec((None, k, tn), lambda i, j: (layer, 0, j)),
                  pl.BlockSpec((tm, tn), lambda i, j: (i, j))],
        out_specs=pl.BlockSpec((tm, tn), lambda i, j: (i, j)),
        compiler_params=pltpu.CompilerParams(
            dimension_semantics=("parallel", "arbitrary"),
            vmem_limit_bytes=V7X_VMEM_LIMIT_BYTES),
        name="out_proj_residual",
    )(a, w, x)


def _ffn_kernel(*refs, final_norm):
    if final_norm:
        x_ref, g_ref, wg_ref, wu_ref, wd_ref, fg_ref, o_ref, h_ref, acc_ref = refs
    else:
        x_ref, g_ref, wg_ref, wu_ref, wd_ref, o_ref, h_ref, acc_ref = refs
    j = pl.program_id(1)

    @pl.when(j == 0)
    def _():
        h_ref[...] = _rms(x_ref[...], g_ref[...]).astype(BF16)
        acc_ref[...] = jnp.zeros_like(acc_ref)

    h = h_ref[...]
    a = jax.nn.silu(_dot(h, wg_ref[...])) * _dot(h, wu_ref[...])
    acc_ref[...] += _dot(a.astype(BF16), wd_ref[...])

    @pl.when(j == pl.num_programs(1) - 1)
    def _():
        y = x_ref[...] + acc_ref[...]
        if final_norm:
            y = _rms(y, fg_ref[...])
        o_ref[...] = y


def _ffn(x, g, wg, wu, wd, layer, final_g, *, tm, tf):
    m, d = x.shape
    dff = wg.shape[2]
    tm, tf = _tile(m, tm), _tile(dff, tf)
    vec = pl.BlockSpec((1, d), lambda i, j: (0, 0))
    in_specs = [
        pl.BlockSpec((tm, d), lambda i, j: (i, 0)),
        vec,
        pl.BlockSpec((None, d, tf), lambda i, j: (layer, 0, j)),
        pl.BlockSpec((None, d, tf), lambda i, j: (layer, 0, j)),
        pl.BlockSpec((None, tf, d), lambda i, j: (layer, j, 0)),
    ]
    args = [x, g.reshape(1, d), wg, wu, wd]
    if final_g is not None:
        in_specs.append(vec)
        args.append(final_g.reshape(1, d))
    return pl.pallas_call(
        functools.partial(_ffn_kernel, final_norm=final_g is not None),
        out_shape=jax.ShapeDtypeStruct((m, d), F32),
        grid=(m // tm, dff // tf),
        in_specs=in_specs,
        out_specs=pl.BlockSpec((tm, d), lambda i, j: (i, 0)),
        scratch_shapes=[pltpu.VMEM((tm, d), BF16), pltpu.VMEM((tm, d), F32)],
        compiler_params=pltpu.CompilerParams(
            dimension_semantics=("parallel", "arbitrary"),
            vmem_limit_bytes=V7X_VMEM_LIMIT_BYTES),
        name="swiglu_ffn",
    )(*args)


def kernel(x, norm_mix_g, norm_ffn_g, w_in, b_gate, lam_q1, lam_k1, lam_q2, lam_k2, subln_w,
           w_branch, w_out, w_ffn_gate, w_ffn_up, w_ffn_down, final_norm_g):
    batch, seq, d = x.shape
    depth = w_in.shape[0]
    n_qkv = 3 * N_BRANCH * MIX_W
    assert w_in.shape[2] == n_qkv + N_BRANCH * d
    xf = x.reshape(batch * seq, d)
    w_in, w_branch, w_out, w_ffn_gate, w_ffn_up, w_ffn_down = (
        w.astype(BF16) for w in (w_in, w_branch, w_out, w_ffn_gate, w_ffn_up, w_ffn_down))
    for l in range(depth):
        lambda_init = 0.8 - 0.6 * math.exp(-0.3 * l)
        qkv, h = _norm_matmul(xf, norm_mix_g[l], w_in, l, n_qkv, tm=1024, tn=1024,
                              name="norm_qkv_proj")
        oa = _diff_attention(qkv, lam_q1[l], lam_k1[l], lam_q2[l], lam_k2[l], subln_w[l],
                             lambda_init, batch, seq, tq=256, tk=256, hb=8)
        ob = _sb_attention(qkv, batch, seq, tq=256, tk=256, hb=8)
        oc = _moba_attention(qkv, batch, seq, tq=256, tk=256, hb=8)
        merged = _gate_merge(h, oa, ob, oc, w_in, b_gate, w_branch, l, n_qkv, tm=1024, tn=512)
        xf = _out_proj(merged, w_out, l, xf, tm=1024, tn=1024)
        xf = _ffn(xf, norm_ffn_g[l], w_ffn_gate, w_ffn_up, w_ffn_down, l,
                  final_norm_g if l == depth - 1 else None, tm=512, tf=512)
    return xf.reshape(batch, seq, d)
```

```python
import functools
import math

import jax
import jax.numpy as jnp
import numpy as np
from jax import lax
from jax.experimental import pallas as pl
from jax.experimental.pallas import tpu as pltpu

F32 = jnp.float32
BF16 = jnp.bfloat16

HEAD_DIM = 128
N_HEADS = 8
MIX_W = N_HEADS * HEAD_DIM
N_BRANCH = 3
MOBA_BLOCK = 256
MOBA_TOPK = 3
RMS_EPS = 1e-6
N_ALIBI = 2 * N_HEADS
_ALIBI_ALL = 2.0 ** (-8.0 * (np.arange(N_ALIBI) + 1) / N_ALIBI)
ALIBI_DIFF = _ALIBI_ALL[0::2].astype(np.float32)
ALIBI_MOBA = _ALIBI_ALL[1::2].astype(np.float32)

LOG2E = 1.4426950408889634
V7X_VMEM_LIMIT_BYTES = 56 * 1024 * 1024
MASKED = -1e30
VT_ROWS = HEAD_DIM + 16
EXP2_UNDERFLOW = -151.0


def _tile(n, target):
    best = None
    for c in range(128, min(n, target) + 1, 128):
        if n % c == 0:
            best = c
    assert best is not None, (n, target)
    return best


def _dot(a, b):
    return jnp.dot(a, b, preferred_element_type=F32)


def _dot_nt(a, b):
    return lax.dot_general(a, b, (((1,), (1,)), ((), ())), preferred_element_type=F32)


def _rms(x, g):
    ms = jnp.mean(x * x, axis=-1, keepdims=True)
    return x * lax.rsqrt(ms + RMS_EPS) * g


def _norm_matmul_kernel(x_ref, g_ref, w_ref, o_ref, h_ref):
    @pl.when(pl.program_id(1) == 0)
    def _():
        h_ref[...] = _rms(x_ref[...], g_ref[...]).astype(BF16)

    o_ref[...] = _dot(h_ref[...], w_ref[...]).astype(o_ref.dtype)


def _norm_matmul(x, g, w, layer, n, *, tm, tn, name):
    m, d = x.shape
    tm, tn = _tile(m, tm), _tile(n, tn)
    return pl.pallas_call(
        _norm_matmul_kernel,
        out_shape=(jax.ShapeDtypeStruct((m, n), BF16), jax.ShapeDtypeStruct((m, d), BF16)),
        grid=(m // tm, n // tn),
        in_specs=[
            pl.BlockSpec((tm, d), lambda i, j: (i, 0)),
            pl.BlockSpec((1, d), lambda i, j: (0, 0)),
            pl.BlockSpec((None, d, tn), lambda i, j: (layer, 0, j)),
        ],
        out_specs=(pl.BlockSpec((tm, tn), lambda i, j: (i, j)),
                   pl.BlockSpec((tm, d), lambda i, j: (i, 0))),
        compiler_params=pltpu.CompilerParams(
            dimension_semantics=("parallel", "arbitrary"),
            vmem_limit_bytes=V7X_VMEM_LIMIT_BYTES),
        name=name,
    )(x, g.reshape(1, d), w)


def _attn_specs(seq, tq, hb, col_q, col_k, col_v):
    nq = seq // tq
    w = hb * HEAD_DIM
    return [
        pl.BlockSpec((tq, w), lambda b, g, i: (b * nq + i, col_q // hb + g)),
        pl.BlockSpec((seq, w), lambda b, g, i: (b, col_k // hb + g)),
        pl.BlockSpec((seq, w), lambda b, g, i: (b, col_v // hb + g)),
    ]


def _attn_out_spec(seq, tq, hb):
    nq = seq // tq
    return pl.BlockSpec((tq, hb * HEAD_DIM), lambda b, g, i: (b * nq + i, g))


_ATTN_PARAMS = pltpu.CompilerParams(
    dimension_semantics=("parallel", "parallel", "arbitrary"),
    vmem_limit_bytes=V7X_VMEM_LIMIT_BYTES)


def _head(ref, s, rows=slice(None)):
    return ref[rows, s * HEAD_DIM:(s + 1) * HEAD_DIM]


def _transpose_v(v_ref, vt_ref, hb, ones_rows=False):
    for s in range(hb):
        vt_ref[s, :HEAD_DIM, :] = _head(v_ref, s).T
        if ones_rows:
            vt_ref[s, HEAD_DIM:, :] = jnp.ones((VT_ROWS - HEAD_DIM, vt_ref.shape[2]), vt_ref.dtype)


def _bias_columns(rows, slope):
    lane = lax.broadcasted_iota(jnp.int32, (rows, HEAD_DIM), 1)
    x = jnp.full((rows, HEAD_DIM), slope, F32)
    hi = x.astype(BF16).astype(F32)
    mid = (x - hi).astype(BF16).astype(F32)
    lo = x - hi - mid
    cols = jnp.where(lane == 0, hi, jnp.where(lane == 1, mid, jnp.where(lane == 2, lo, 0.0)))
    return cols.astype(BF16)


def _position_columns(tk):
    lane = lax.broadcasted_iota(jnp.int32, (tk, HEAD_DIM), 1)
    c = lax.broadcasted_iota(jnp.int32, (tk, HEAD_DIM), 0)
    return jnp.where(lane < 3, c, 0).astype(F32).astype(BF16)


def _key_query_index(tk, n, tq):
    c = lax.broadcasted_iota(jnp.int32, (tk, n), 0)
    r = lax.broadcasted_iota(jnp.int32, (tk, n), 1)
    if n != tq:
        r = r % tq
    return c, r


def _softmax_tiles(i, tq, tk, n, vt_ref, scores, stat_bias=None):
    n_full = i * (tq // tk)
    kidx, qidx = _key_query_index(tk, n, tq)
    heads = range(len(scores))

    def tiles(j, carry, diag):
        k0 = pl.multiple_of(j * tk, tk)
        us = [scores[s](k0) for s in heads]
        ps = []
        for s in heads:
            m, acc = carry[s]
            u, off = us[s]
            if diag is not None:
                u = jnp.where(kidx + diag * tk <= qidx, u, -jnp.inf)
            if stat_bias is not None:
                off = off + stat_bias[s](k0)
            m_new = jnp.maximum(m, jnp.max(u, axis=0, keepdims=True) + off)
            p = jnp.exp2(u - (m_new - off))
            ps.append((m_new, jnp.exp2(m - m_new), p.astype(BF16)))
        out = []
        for s in heads:
            m_new, alpha, p = ps[s]
            acc = alpha * carry[s][1] + _dot(vt_ref[s, :, pl.ds(k0, tk)], p)
            out.append((m_new, acc))
        return tuple(out)

    carry = tuple((jnp.full((1, n), -jnp.inf, F32), jnp.zeros((VT_ROWS, n), F32)) for _ in heads)
    for d in range(tq // tk):
        carry = tiles(n_full + d, carry, d)
    carry = lax.fori_loop(0, n_full, lambda j, c: tiles(j, c, None), carry)
    return [(acc[:HEAD_DIM], acc[HEAD_DIM:HEAD_DIM + 1]) for _, acc in carry]


def _diff_kernel(slopes_ref, lq1_ref, lk1_ref, lq2_ref, lk2_ref, subw_ref,
                 q_ref, k_ref, v_ref, o_ref, vt_ref, *, lambda_init, tq, tk, hb):
    g = pl.program_id(1)
    i = pl.program_id(2)

    @pl.when(i == 0)
    def _():
        _transpose_v(v_ref, vt_ref, hb, ones_rows=True)

    half = HEAD_DIM // 2
    n = 2 * tq
    pos_cols = _position_columns(tk)
    lane = lax.broadcasted_iota(jnp.int32, (tq, HEAD_DIM), 1)

    def make_scores(s):
        slope2 = slopes_ref[g * hb + s] * LOG2E
        q = _head(q_ref, s) * jnp.asarray(half ** -0.5, BF16)
        zero = jnp.zeros_like(q)
        qq = jnp.concatenate([jnp.where(lane < half, q, zero), jnp.where(lane < half, zero, q)],
                             axis=0)
        qq = jnp.concatenate([qq, _bias_columns(n, slopes_ref[g * hb + s])], axis=1)

        def scores(k0):
            ka = jnp.concatenate([_head(k_ref, s, pl.ds(k0, tk)), pos_cols], axis=1)
            return _dot_nt(ka, qq) * LOG2E, slope2 * (k0 - i * tq).astype(F32)
        return scores

    outs = _softmax_tiles(i, tq, tk, n, vt_ref, [make_scores(s) for s in range(hb)])
    lam = (jnp.exp(jnp.sum(lq1_ref[...] * lk1_ref[...], axis=1, keepdims=True))
           - jnp.exp(jnp.sum(lq2_ref[...] * lk2_ref[...], axis=1, keepdims=True))
           + lambda_init)
    for s, (acc, l) in enumerate(outs):
        o = acc / l
        o = (o[:, :tq] - lam * o[:, tq:]).T
        o = _rms(o, subw_ref[...]) * (1.0 - lambda_init)
        o_ref[:, s * HEAD_DIM:(s + 1) * HEAD_DIM] = o.astype(o_ref.dtype)


def _diff_attention(qkv, lq1, lk1, lq2, lk2, subw, lambda_init, batch, seq, *, tq, tk, hb):
    half = HEAD_DIM // 2
    small = pl.BlockSpec((1, half), lambda b, g, i: (0, 0))
    return pl.pallas_call(
        functools.partial(_diff_kernel, lambda_init=lambda_init, tq=tq, tk=tk, hb=hb),
        out_shape=jax.ShapeDtypeStruct((batch * seq, MIX_W), BF16),
        grid=(batch, N_HEADS // hb, seq // tq),
        in_specs=[pl.BlockSpec(memory_space=pltpu.SMEM), small, small, small, small,
                  pl.BlockSpec((1, HEAD_DIM), lambda b, g, i: (0, 0))]
                 + _attn_specs(seq, tq, hb, 0, N_HEADS, 2 * N_HEADS),
        out_specs=_attn_out_spec(seq, tq, hb),
        scratch_shapes=[pltpu.VMEM((hb, VT_ROWS, seq), BF16)],
        compiler_params=_ATTN_PARAMS,
        name="diff_attention",
    )(jnp.asarray(ALIBI_DIFF), lq1.reshape(1, half), lk1.reshape(1, half),
      lq2.reshape(1, half), lk2.reshape(1, half), subw.reshape(1, HEAD_DIM), qkv, qkv, qkv)


def _sb_kernel(q_ref, k_ref, v_ref, o_ref, vt_ref, *, tq, tk, hb):
    i = pl.program_id(2)

    @pl.when(i == 0)
    def _():
        _transpose_v(v_ref, vt_ref, hb)

    scale2 = HEAD_DIM ** -0.5 * LOG2E
    n_full = i * (tq // tk)
    kidx, qidx = _key_query_index(tk, tq, tq)
    a = lax.broadcasted_iota(jnp.int32, (tk + 16, tk), 0)
    b = lax.broadcasted_iota(jnp.int32, (tk + 16, tk), 1)
    later = jnp.where((b > a) | (a >= tk), 1.0, 0.0).astype(BF16)
    qs = [_head(q_ref, s) for s in range(hb)]
    heads = range(hb)

    def tiles(j, carry, diag):
        masked = diag is not None
        k0 = pl.multiple_of(j * tk, tk)
        zs = [_dot_nt(_head(k_ref, s, pl.ds(k0, tk)), qs[s]) * scale2 for s in heads]
        if masked:
            strict = kidx + diag * tk < qidx
        mid = []
        for s in heads:
            z = zs[s]
            sp = jnp.log2(1.0 + jnp.exp2(-jnp.abs(z)))
            log_beta = jnp.minimum(z, 0.0) - sp
            log_fail = -jnp.maximum(z, 0.0) - sp
            if masked:
                log_fail = jnp.where(strict, log_fail, 0.0)
            hi = log_fail.astype(BF16)
            lo = (log_fail - hi.astype(F32)).astype(BF16)
            mid.append((log_beta, hi, lo))
        sufs = [_dot(later, mid[s][1]) + _dot(later, mid[s][2]) for s in heads]
        ws = []
        for s in heads:
            w = jnp.exp2(mid[s][0] + sufs[s][:tk] + carry[s][1])
            if masked:
                w = jnp.where(strict, w, 0.0)
            ws.append(w.astype(BF16))
        out = []
        for s in heads:
            acc = carry[s][0] + _dot(vt_ref[s, :, pl.ds(k0, tk)], ws[s])
            run = carry[s][1] + sufs[s][tk:tk + 1]
            out.append((acc, run))
        return tuple(out)

    def alive(carry):
        top = carry[0][1]
        for s in heads[1:]:
            top = jnp.maximum(top, carry[s][1])
        return jnp.max(top) >= EXP2_UNDERFLOW

    carry = tuple((jnp.zeros((HEAD_DIM, tq), F32), jnp.zeros((1, tq), F32)) for _ in heads)
    for d in reversed(range(tq // tk)):
        carry = tiles(n_full + d, carry, d)

    def body(state):
        t, _, c = state
        c = tiles(n_full - 1 - t, c, None)
        return t + 1, alive(c), c

    _, _, carry = lax.while_loop(lambda st: jnp.logical_and(st[0] < n_full, st[1]), body,
                                 (jnp.int32(0), alive(carry), carry))
    for s in heads:
        o_ref[:, s * HEAD_DIM:(s + 1) * HEAD_DIM] = carry[s][0].T.astype(o_ref.dtype)


def _sb_attention(qkv, batch, seq, *, tq, tk, hb):
    return pl.pallas_call(
        functools.partial(_sb_kernel, tq=tq, tk=tk, hb=hb),
        out_shape=jax.ShapeDtypeStruct((batch * seq, MIX_W), BF16),
        grid=(batch, N_HEADS // hb, seq // tq),
        in_specs=_attn_specs(seq, tq, hb, 3 * N_HEADS, 4 * N_HEADS, 5 * N_HEADS),
        out_specs=_attn_out_spec(seq, tq, hb),
        scratch_shapes=[pltpu.VMEM((hb, HEAD_DIM, seq), BF16)],
        compiler_params=_ATTN_PARAMS,
        name="stick_breaking_attention",
    )(qkv, qkv, qkv)


def _moba_kernel(slopes_ref, q_ref, k_ref, v_ref, o_ref, vt_ref, kmean_ref, selb_ref,
                 *, tq, tk, hb):
    nb = k_ref.shape[0] // MOBA_BLOCK
    g = pl.program_id(1)
    i = pl.program_id(2)
    own = (i * tq) // MOBA_BLOCK

    @pl.when(i == 0)
    def _():
        _transpose_v(v_ref, vt_ref, hb, ones_rows=True)
        for s in range(hb):
            kf = _head(k_ref, s).astype(F32).reshape(nb, MOBA_BLOCK, HEAD_DIM)
            kmean_ref[s] = jnp.mean(kf, axis=1)

    c1 = HEAD_DIM ** -0.5 * LOG2E
    n_idx = lax.broadcasted_iota(jnp.int32, (nb, tq), 0)
    pos_cols = _position_columns(tk)

    def make_head(s):
        slope2 = slopes_ref[g * hb + s] * LOG2E
        q = _head(q_ref, s)
        km = kmean_ref[s]
        km_hi = km.astype(BF16)
        km_lo = (km - km_hi.astype(F32)).astype(BF16)
        gate = _dot_nt(km_hi, q) + _dot_nt(km_lo, q)
        beaten = jnp.zeros((nb, tq), jnp.int32)
        for mth in range(nb):
            gm = gate[mth:mth + 1, :]
            beats = (gm > gate) | ((gm == gate) & (mth < n_idx))
            beaten = beaten + jnp.where(beats & (mth < own), 1, 0)
        visible = (n_idx == own) | ((n_idx < own) & (beaten < MOBA_TOPK))
        selb_ref[s] = jnp.where(visible, 0.0, MASKED)
        qa = jnp.concatenate([q, _bias_columns(tq, slopes_ref[g * hb + s] * HEAD_DIM ** 0.5)], axis=1)

        def scores(k0):
            ka = jnp.concatenate([_head(k_ref, s, pl.ds(k0, tk)), pos_cols], axis=1)
            return _dot_nt(ka, qa) * c1, slope2 * (k0 - i * tq).astype(F32)

        def block_bias(k0):
            return selb_ref[s, pl.ds(k0 // MOBA_BLOCK, 1), :]
        return scores, block_bias

    fns = [make_head(s) for s in range(hb)]
    outs = _softmax_tiles(i, tq, tk, tq, vt_ref, [f[0] for f in fns], [f[1] for f in fns])
    for s, (acc, l) in enumerate(outs):
        o_ref[:, s * HEAD_DIM:(s + 1) * HEAD_DIM] = (acc / l).T.astype(o_ref.dtype)


def _moba_attention(qkv, batch, seq, *, tq, tk, hb):
    assert seq % MOBA_BLOCK == 0 and MOBA_BLOCK % tk == 0 and MOBA_BLOCK % tq == 0
    nb = seq // MOBA_BLOCK
    return pl.pallas_call(
        functools.partial(_moba_kernel, tq=tq, tk=tk, hb=hb),
        out_shape=jax.ShapeDtypeStruct((batch * seq, MIX_W), BF16),
        grid=(batch, N_HEADS // hb, seq // tq),
        in_specs=[pl.BlockSpec(memory_space=pltpu.SMEM)]
                 + _attn_specs(seq, tq, hb, 6 * N_HEADS, 7 * N_HEADS, 8 * N_HEADS),
        out_specs=_attn_out_spec(seq, tq, hb),
        scratch_shapes=[pltpu.VMEM((hb, VT_ROWS, seq), BF16),
                        pltpu.VMEM((hb, nb, HEAD_DIM), F32),
                        pltpu.VMEM((hb, nb, tq), F32)],
        compiler_params=_ATTN_PARAMS,
        name="moba_attention",
    )(jnp.asarray(ALIBI_MOBA), qkv, qkv, qkv)


def _gate_merge_kernel(h_ref, oa_ref, ob_ref, oc_ref, wga_ref, wgb_ref, wgc_ref,
                       ba_ref, bb_ref, bc_ref, wbr_ref, o_ref):
    h = h_ref[...]
    merged = None
    for n, (o_r, wg_r, b_r) in enumerate(((oa_ref, wga_ref, ba_ref), (ob_ref, wgb_ref, bb_ref),
                                          (oc_ref, wgc_ref, bc_ref))):
        gate = jax.nn.sigmoid(_dot(h, wg_r[...]) + b_r[...])
        term = gate * _dot(o_r[...], wbr_ref[n])
        merged = term if merged is None else merged + term
    o_ref[...] = merged.astype(o_ref.dtype)


def _gate_merge(h, oa, ob, oc, w_in, b_gate, w_branch, layer, gate_col0, *, tm, tn):
    m, d = h.shape
    tm, tn = _tile(m, tm), _tile(d, tn)
    nj = d // tn
    assert gate_col0 % tn == 0
    c0 = gate_col0 // tn
    o_spec = pl.BlockSpec((tm, MIX_W), lambda i, j: (i, 0))
    wg_specs = [pl.BlockSpec((None, d, tn),
                             functools.partial(lambda i, j, n: (layer, 0, c0 + n * nj + j), n=n))
                for n in range(N_BRANCH)]
    b_specs = [pl.BlockSpec((None, 1, tn),
                            functools.partial(lambda i, j, n: (layer, 0, n * nj + j), n=n))
               for n in range(N_BRANCH)]
    b_gate = b_gate.reshape(b_gate.shape[0], 1, -1)
    return pl.pallas_call(
        _gate_merge_kernel,
        out_shape=jax.ShapeDtypeStruct((m, d), BF16),
        grid=(m // tm, nj),
        in_specs=[pl.BlockSpec((tm, d), lambda i, j: (i, 0)), o_spec, o_spec, o_spec]
                 + wg_specs + b_specs
                 + [pl.BlockSpec((None, N_BRANCH, MIX_W, tn), lambda i, j: (layer, 0, 0, j))],
        out_specs=pl.BlockSpec((tm, tn), lambda i, j: (i, j)),
        compiler_params=pltpu.CompilerParams(
            dimension_semantics=("parallel", "arbitrary"),
            vmem_limit_bytes=V7X_VMEM_LIMIT_BYTES),
        name="gate_merge",
    )(h, oa, ob, oc, w_in, w_in, w_in, b_gate, b_gate, b_gate, w_branch)


def _out_proj_kernel(a_ref, w_ref, x_ref, o_ref):
    o_ref[...] = x_ref[...] + _dot(a_ref[...], w_ref[...])


def _out_proj(a, w, layer, x, *, tm, tn):
    m, d = x.shape
    k = a.shape[1]
    tm, tn = _tile(m, tm), _tile(d, tn)
    return pl.pallas_call(
        _out_proj_kernel,
        out_shape=jax.ShapeDtypeStruct((m, d), F32),
        grid=(m // tm, d // tn),
        in_specs=[pl.BlockSpec((tm, k), lambda i, j: (i, 0)),
                  pl.BlockSpec((None, k, tn), lambda i, j: (layer, 0, j)),
                  pl.BlockSpec((tm, tn), lambda i, j: (i, j))],
        out_specs=pl.BlockSpec((tm, tn), lambda i, j: (i, j)),
        compiler_params=pltpu.CompilerParams(
            dimension_semantics=("parallel", "arbitrary"),
            vmem_limit_bytes=V7X_VMEM_LIMIT_BYTES),
        name="out_proj_residual",
    )(a, w, x)


def _ffn_kernel(*refs, final_norm):
    if final_norm:
        x_ref, g_ref, wg_ref, wu_ref, wd_ref, fg_ref, o_ref, h_ref = refs
    else:
        x_ref, g_ref, wg_ref, wu_ref, wd_ref, o_ref, h_ref = refs
    j = pl.program_id(1)

    @pl.when(j == 0)
    def _():
        x = x_ref[...]
        h_ref[...] = _rms(x, g_ref[...]).astype(BF16)
        o_ref[...] = x

    h = h_ref[...]
    a = jax.nn.silu(_dot(h, wg_ref[...])) * _dot(h, wu_ref[...])
    o_ref[...] += _dot(a.astype(BF16), wd_ref[...])

    if final_norm:
        @pl.when(j == pl.num_programs(1) - 1)
        def _():
            o_ref[...] = _rms(o_ref[...], fg_ref[...])


def _ffn(x, g, wg, wu, wd, layer, final_g, *, tm, tf):
    m, d = x.shape
    dff = wg.shape[2]
    tm, tf = _tile(m, tm), _tile(dff, tf)
    vec = pl.BlockSpec((1, d), lambda i, j: (0, 0))
    in_specs = [
        pl.BlockSpec((tm, d), lambda i, j: (i, 0)),
        vec,
        pl.BlockSpec((None, d, tf), lambda i, j: (layer, 0, j)),
        pl.BlockSpec((None, d, tf), lambda i, j: (layer, 0, j)),
        pl.BlockSpec((None, tf, d), lambda i, j: (layer, j, 0)),
    ]
    args = [x, g.reshape(1, d), wg, wu, wd]
    if final_g is not None:
        in_specs.append(vec)
        args.append(final_g.reshape(1, d))
    return pl.pallas_call(
        functools.partial(_ffn_kernel, final_norm=final_g is not None),
        out_shape=jax.ShapeDtypeStruct((m, d), F32),
        grid=(m // tm, dff // tf),
        in_specs=in_specs,
        out_specs=pl.BlockSpec((tm, d), lambda i, j: (i, 0)),
        scratch_shapes=[pltpu.VMEM((tm, d), BF16)],
        compiler_params=pltpu.CompilerParams(
            dimension_semantics=("parallel", "arbitrary"),
            vmem_limit_bytes=V7X_VMEM_LIMIT_BYTES),
        name="swiglu_ffn",
    )(*args)


def kernel(x, norm_mix_g, norm_ffn_g, w_in, b_gate, lam_q1, lam_k1, lam_q2, lam_k2, subln_w,
           w_branch, w_out, w_ffn_gate, w_ffn_up, w_ffn_down, final_norm_g):
    batch, seq, d = x.shape
    depth = w_in.shape[0]
    n_qkv = 3 * N_BRANCH * MIX_W
    assert w_in.shape[2] == n_qkv + N_BRANCH * d
    xf = x.reshape(batch * seq, d)
    w_in, w_branch, w_out, w_ffn_gate, w_ffn_up, w_ffn_down = (
        w.astype(BF16) for w in (w_in, w_branch, w_out, w_ffn_gate, w_ffn_up, w_ffn_down))
    for l in range(depth):
        lambda_init = 0.8 - 0.6 * math.exp(-0.3 * l)
        qkv, h = _norm_matmul(xf, norm_mix_g[l], w_in, l, n_qkv, tm=1024, tn=2304,
                              name="norm_qkv_proj")
        oa = _diff_attention(qkv, lam_q1[l], lam_k1[l], lam_q2[l], lam_k2[l], subln_w[l],
                             lambda_init, batch, seq, tq=256, tk=256, hb=8)
        ob = _sb_attention(qkv, batch, seq, tq=256, tk=256, hb=8)
        oc = _moba_attention(qkv, batch, seq, tq=256, tk=256, hb=8)
        merged = _gate_merge(h, oa, ob, oc, w_in, b_gate, w_branch, l, n_qkv, tm=1024, tn=512)
        xf = _out_proj(merged, w_out, l, xf, tm=1024, tn=1024)
        xf = _ffn(xf, norm_ffn_g[l], w_ffn_gate, w_ffn_up, w_ffn_down, l,
                  final_norm_g if l == depth - 1 else None, tm=1024, tf=512)
    return xf.reshape(batch, seq, d)
```

```python
import functools
import math

import jax
import jax.numpy as jnp
import numpy as np
from jax import lax
from jax.experimental import pallas as pl
from jax.experimental.pallas import tpu as pltpu

F32 = jnp.float32
BF16 = jnp.bfloat16

HEAD_DIM = 128
N_HEADS = 8
MIX_W = N_HEADS * HEAD_DIM
N_BRANCH = 3
MOBA_BLOCK = 256
MOBA_TOPK = 3
RMS_EPS = 1e-6
N_ALIBI = 2 * N_HEADS
_ALIBI_ALL = 2.0 ** (-8.0 * (np.arange(N_ALIBI) + 1) / N_ALIBI)
ALIBI_DIFF = _ALIBI_ALL[0::2].astype(np.float32)
ALIBI_MOBA = _ALIBI_ALL[1::2].astype(np.float32)

LOG2E = 1.4426950408889634
V7X_VMEM_LIMIT_BYTES = 56 * 1024 * 1024
MASKED = -1e30
VT_ROWS = HEAD_DIM + 16
EXP2_UNDERFLOW = -151.0


def _tile(n, target):
    best = None
    for c in range(128, min(n, target) + 1, 128):
        if n % c == 0:
            best = c
    assert best is not None, (n, target)
    return best


def _dot(a, b):
    return jnp.dot(a, b, preferred_element_type=F32)


def _dot_nt(a, b):
    return lax.dot_general(a, b, (((1,), (1,)), ((), ())), preferred_element_type=F32)


def _rms(x, g):
    ms = jnp.mean(x * x, axis=-1, keepdims=True)
    return x * lax.rsqrt(ms + RMS_EPS) * g


def _norm_matmul_kernel(x_ref, g_ref, w_ref, o_ref, h_ref):
    @pl.when(pl.program_id(1) == 0)
    def _():
        h_ref[...] = _rms(x_ref[...], g_ref[...]).astype(BF16)

    o_ref[...] = _dot(h_ref[...], w_ref[...]).astype(o_ref.dtype)


def _norm_matmul(x, g, w, layer, n, *, tm, tn, name):
    m, d = x.shape
    tm, tn = _tile(m, tm), _tile(n, tn)
    return pl.pallas_call(
        _norm_matmul_kernel,
        out_shape=(jax.ShapeDtypeStruct((m, n), BF16), jax.ShapeDtypeStruct((m, d), BF16)),
        grid=(m // tm, n // tn),
        in_specs=[
            pl.BlockSpec((tm, d), lambda i, j: (i, 0)),
            pl.BlockSpec((1, d), lambda i, j: (0, 0)),
            pl.BlockSpec((None, d, tn), lambda i, j: (layer, 0, j)),
        ],
        out_specs=(pl.BlockSpec((tm, tn), lambda i, j: (i, j)),
                   pl.BlockSpec((tm, d), lambda i, j: (i, 0))),
        compiler_params=pltpu.CompilerParams(
            dimension_semantics=("parallel", "arbitrary"),
            vmem_limit_bytes=V7X_VMEM_LIMIT_BYTES),
        name=name,
    )(x, g.reshape(1, d), w)


def _attn_specs(seq, tq, hb, col_q, col_k, col_v):
    nq = seq // tq
    w = hb * HEAD_DIM
    return [
        pl.BlockSpec((tq, w), lambda b, g, i: (b * nq + i, col_q // hb + g)),
        pl.BlockSpec((seq, w), lambda b, g, i: (b, col_k // hb + g)),
        pl.BlockSpec((seq, w), lambda b, g, i: (b, col_v // hb + g)),
    ]


def _attn_out_spec(seq, tq, hb):
    nq = seq // tq
    return pl.BlockSpec((tq, hb * HEAD_DIM), lambda b, g, i: (b * nq + i, g))


_ATTN_PARAMS = pltpu.CompilerParams(
    dimension_semantics=("parallel", "parallel", "arbitrary"),
    vmem_limit_bytes=V7X_VMEM_LIMIT_BYTES)


def _head(ref, s, rows=slice(None)):
    return ref[rows, s * HEAD_DIM:(s + 1) * HEAD_DIM]


def _transpose_v(v_ref, vt_ref, hb, ones_rows=False):
    for s in range(hb):
        vt_ref[s, :HEAD_DIM, :] = _head(v_ref, s).T
        if ones_rows:
            vt_ref[s, HEAD_DIM:, :] = jnp.ones((VT_ROWS - HEAD_DIM, vt_ref.shape[2]), vt_ref.dtype)


def _bias_columns(rows, slope):
    lane = lax.broadcasted_iota(jnp.int32, (rows, HEAD_DIM), 1)
    x = jnp.full((rows, HEAD_DIM), slope, F32)
    hi = x.astype(BF16).astype(F32)
    mid = (x - hi).astype(BF16).astype(F32)
    lo = x - hi - mid
    cols = jnp.where(lane == 0, hi, jnp.where(lane == 1, mid, jnp.where(lane == 2, lo, 0.0)))
    return cols.astype(BF16)


def _position_columns(tk):
    lane = lax.broadcasted_iota(jnp.int32, (tk, HEAD_DIM), 1)
    c = lax.broadcasted_iota(jnp.int32, (tk, HEAD_DIM), 0)
    return jnp.where(lane < 3, c, 0).astype(F32).astype(BF16)


def _key_query_index(tk, n, tq):
    c = lax.broadcasted_iota(jnp.int32, (tk, n), 0)
    r = lax.broadcasted_iota(jnp.int32, (tk, n), 1)
    if n != tq:
        r = r % tq
    return c, r


def _softmax_tiles(i, tq, tk, n, vt_ref, scores, stat_bias=None):
    n_full = i * (tq // tk)
    kidx, qidx = _key_query_index(tk, n, tq)
    heads = range(len(scores))

    def tiles(j, carry, diag):
        k0 = pl.multiple_of(j * tk, tk)
        us = [scores[s](k0) for s in heads]
        ps = []
        for s in heads:
            m, acc = carry[s]
            u, off = us[s]
            if diag is not None:
                u = jnp.where(kidx + diag * tk <= qidx, u, -jnp.inf)
            if stat_bias is not None:
                off = off + stat_bias[s](k0)
            m_new = jnp.maximum(m, jnp.max(u, axis=0, keepdims=True) + off)
            p = jnp.exp2(u - (m_new - off))
            ps.append((m_new, jnp.exp2(m - m_new), p.astype(BF16)))
        out = []
        for s in heads:
            m_new, alpha, p = ps[s]
            acc = alpha * carry[s][1] + _dot(vt_ref[s, :, pl.ds(k0, tk)], p)
            out.append((m_new, acc))
        return tuple(out)

    carry = tuple((jnp.full((1, n), -jnp.inf, F32), jnp.zeros((VT_ROWS, n), F32)) for _ in heads)
    for d in range(tq // tk):
        carry = tiles(n_full + d, carry, d)
    carry = lax.fori_loop(0, n_full, lambda j, c: tiles(j, c, None), carry)
    return [(acc[:HEAD_DIM], acc[HEAD_DIM:HEAD_DIM + 1]) for _, acc in carry]


def _diff_kernel(slopes_ref, lq1_ref, lk1_ref, lq2_ref, lk2_ref, subw_ref,
                 q_ref, k_ref, v_ref, o_ref, vt_ref, *, lambda_init, tq, tk, hb):
    g = pl.program_id(1)
    i = pl.program_id(2)

    @pl.when(i == 0)
    def _():
        _transpose_v(v_ref, vt_ref, hb, ones_rows=True)

    half = HEAD_DIM // 2
    n = 2 * tq
    pos_cols = _position_columns(tk)
    lane = lax.broadcasted_iota(jnp.int32, (tq, HEAD_DIM), 1)

    def make_scores(s):
        slope2 = slopes_ref[g * hb + s] * LOG2E
        q = _head(q_ref, s) * jnp.asarray(half ** -0.5, BF16)
        zero = jnp.zeros_like(q)
        qq = jnp.concatenate([jnp.where(lane < half, q, zero), jnp.where(lane < half, zero, q)],
                             axis=0)
        qq = jnp.concatenate([qq, _bias_columns(n, slopes_ref[g * hb + s])], axis=1)

        def scores(k0):
            ka = jnp.concatenate([_head(k_ref, s, pl.ds(k0, tk)), pos_cols], axis=1)
            return _dot_nt(ka, qq) * LOG2E, slope2 * (k0 - i * tq).astype(F32)
        return scores

    outs = _softmax_tiles(i, tq, tk, n, vt_ref, [make_scores(s) for s in range(hb)])
    lam = (jnp.exp(jnp.sum(lq1_ref[...] * lk1_ref[...], axis=1, keepdims=True))
           - jnp.exp(jnp.sum(lq2_ref[...] * lk2_ref[...], axis=1, keepdims=True))
           + lambda_init)
    for s, (acc, l) in enumerate(outs):
        o = acc / l
        o = (o[:, :tq] - lam * o[:, tq:]).T
        o = _rms(o, subw_ref[...]) * (1.0 - lambda_init)
        o_ref[:, s * HEAD_DIM:(s + 1) * HEAD_DIM] = o.astype(o_ref.dtype)


def _diff_attention(qkv, lq1, lk1, lq2, lk2, subw, lambda_init, batch, seq, *, tq, tk, hb):
    half = HEAD_DIM // 2
    small = pl.BlockSpec((1, half), lambda b, g, i: (0, 0))
    return pl.pallas_call(
        functools.partial(_diff_kernel, lambda_init=lambda_init, tq=tq, tk=tk, hb=hb),
        out_shape=jax.ShapeDtypeStruct((batch * seq, MIX_W), BF16),
        grid=(batch, N_HEADS // hb, seq // tq),
        in_specs=[pl.BlockSpec(memory_space=pltpu.SMEM), small, small, small, small,
                  pl.BlockSpec((1, HEAD_DIM), lambda b, g, i: (0, 0))]
                 + _attn_specs(seq, tq, hb, 0, N_HEADS, 2 * N_HEADS),
        out_specs=_attn_out_spec(seq, tq, hb),
        scratch_shapes=[pltpu.VMEM((hb, VT_ROWS, seq), BF16)],
        compiler_params=_ATTN_PARAMS,
        name="diff_attention",
    )(jnp.asarray(ALIBI_DIFF), lq1.reshape(1, half), lk1.reshape(1, half),
      lq2.reshape(1, half), lk2.reshape(1, half), subw.reshape(1, HEAD_DIM), qkv, qkv, qkv)


def _sb_kernel(q_ref, k_ref, v_ref, o_ref, vt_ref, *, tq, tk, hb):
    i = pl.program_id(2)

    @pl.when(i == 0)
    def _():
        _transpose_v(v_ref, vt_ref, hb)

    scale2 = HEAD_DIM ** -0.5 * LOG2E
    n_full = i * (tq // tk)
    kidx, qidx = _key_query_index(tk, tq, tq)
    a = lax.broadcasted_iota(jnp.int32, (tk + 16, tk), 0)
    b = lax.broadcasted_iota(jnp.int32, (tk + 16, tk), 1)
    later = jnp.where((b > a) | (a >= tk), 1.0, 0.0).astype(BF16)
    qs = [_head(q_ref, s) for s in range(hb)]
    heads = range(hb)

    def tiles(j, carry, diag):
        masked = diag is not None
        k0 = pl.multiple_of(j * tk, tk)
        zs = [_dot_nt(_head(k_ref, s, pl.ds(k0, tk)), qs[s]) * scale2 for s in heads]
        if masked:
            strict = kidx + diag * tk < qidx
        mid = []
        for s in heads:
            z = zs[s]
            sp = jnp.log2(1.0 + jnp.exp2(-jnp.abs(z)))
            log_beta = jnp.minimum(z, 0.0) - sp
            log_fail = -jnp.maximum(z, 0.0) - sp
            if masked:
                log_fail = jnp.where(strict, log_fail, 0.0)
            hi = log_fail.astype(BF16)
            lo = (log_fail - hi.astype(F32)).astype(BF16)
            mid.append((log_beta, hi, lo))
        sufs = [_dot(later, mid[s][1]) + _dot(later, mid[s][2]) for s in heads]
        ws = []
        for s in heads:
            w = jnp.exp2(mid[s][0] + sufs[s][:tk] + carry[s][1])
            if masked:
                w = jnp.where(strict, w, 0.0)
            ws.append(w.astype(BF16))
        out = []
        for s in heads:
            acc = carry[s][0] + _dot(vt_ref[s, :, pl.ds(k0, tk)], ws[s])
            run = carry[s][1] + sufs[s][tk:tk + 1]
            out.append((acc, run))
        return tuple(out)

    def alive(carry):
        top = carry[0][1]
        for s in heads[1:]:
            top = jnp.maximum(top, carry[s][1])
        return jnp.max(top) >= EXP2_UNDERFLOW

    carry = tuple((jnp.zeros((HEAD_DIM, tq), F32), jnp.zeros((1, tq), F32)) for _ in heads)
    for d in reversed(range(tq // tk)):
        carry = tiles(n_full + d, carry, d)

    def body(state):
        t, _, c = state
        c = tiles(n_full - 1 - t, c, None)
        return t + 1, alive(c), c

    _, _, carry = lax.while_loop(lambda st: jnp.logical_and(st[0] < n_full, st[1]), body,
                                 (jnp.int32(0), alive(carry), carry))
    for s in heads:
        o_ref[:, s * HEAD_DIM:(s + 1) * HEAD_DIM] = carry[s][0].T.astype(o_ref.dtype)


def _sb_attention(qkv, batch, seq, *, tq, tk, hb):
    return pl.pallas_call(
        functools.partial(_sb_kernel, tq=tq, tk=tk, hb=hb),
        out_shape=jax.ShapeDtypeStruct((batch * seq, MIX_W), BF16),
        grid=(batch, N_HEADS // hb, seq // tq),
        in_specs=_attn_specs(seq, tq, hb, 3 * N_HEADS, 4 * N_HEADS, 5 * N_HEADS),
        out_specs=_attn_out_spec(seq, tq, hb),
        scratch_shapes=[pltpu.VMEM((hb, HEAD_DIM, seq), BF16)],
        compiler_params=_ATTN_PARAMS,
        name="stick_breaking_attention",
    )(qkv, qkv, qkv)


def _moba_kernel(slopes_ref, q_ref, k_ref, v_ref, o_ref, vt_ref, kmean_ref, selb_ref,
                 *, tq, tk, hb):
    nb = k_ref.shape[0] // MOBA_BLOCK
    g = pl.program_id(1)
    i = pl.program_id(2)
    own = (i * tq) // MOBA_BLOCK

    @pl.when(i == 0)
    def _():
        _transpose_v(v_ref, vt_ref, hb, ones_rows=True)
        for s in range(hb):
            kf = _head(k_ref, s).astype(F32).reshape(nb, MOBA_BLOCK, HEAD_DIM)
            kmean_ref[s] = jnp.mean(kf, axis=1)

    c1 = HEAD_DIM ** -0.5 * LOG2E
    n_idx = lax.broadcasted_iota(jnp.int32, (nb, tq), 0)
    pos_cols = _position_columns(tk)

    def make_head(s):
        slope2 = slopes_ref[g * hb + s] * LOG2E
        q = _head(q_ref, s)
        km = kmean_ref[s]
        km_hi = km.astype(BF16)
        km_lo = (km - km_hi.astype(F32)).astype(BF16)
        gate = _dot_nt(km_hi, q) + _dot_nt(km_lo, q)
        beaten = jnp.zeros((nb, tq), jnp.int32)
        for mth in range(nb):
            gm = gate[mth:mth + 1, :]
            beats = (gm > gate) | ((gm == gate) & (mth < n_idx))
            beaten = beaten + jnp.where(beats & (mth < own), 1, 0)
        visible = (n_idx == own) | ((n_idx < own) & (beaten < MOBA_TOPK))
        selb_ref[s] = jnp.where(visible, 0.0, MASKED)
        qa = jnp.concatenate([q, _bias_columns(tq, slopes_ref[g * hb + s] * HEAD_DIM ** 0.5)], axis=1)

        def scores(k0):
            ka = jnp.concatenate([_head(k_ref, s, pl.ds(k0, tk)), pos_cols], axis=1)
            return _dot_nt(ka, qa) * c1, slope2 * (k0 - i * tq).astype(F32)

        def block_bias(k0):
            return selb_ref[s, pl.ds(k0 // MOBA_BLOCK, 1), :]
        return scores, block_bias

    fns = [make_head(s) for s in range(hb)]
    outs = _softmax_tiles(i, tq, tk, tq, vt_ref, [f[0] for f in fns], [f[1] for f in fns])
    for s, (acc, l) in enumerate(outs):
        o_ref[:, s * HEAD_DIM:(s + 1) * HEAD_DIM] = (acc / l).T.astype(o_ref.dtype)


def _moba_attention(qkv, batch, seq, *, tq, tk, hb):
    assert seq % MOBA_BLOCK == 0 and MOBA_BLOCK % tk == 0 and MOBA_BLOCK % tq == 0
    nb = seq // MOBA_BLOCK
    return pl.pallas_call(
        functools.partial(_moba_kernel, tq=tq, tk=tk, hb=hb),
        out_shape=jax.ShapeDtypeStruct((batch * seq, MIX_W), BF16),
        grid=(batch, N_HEADS // hb, seq // tq),
        in_specs=[pl.BlockSpec(memory_space=pltpu.SMEM)]
                 + _attn_specs(seq, tq, hb, 6 * N_HEADS, 7 * N_HEADS, 8 * N_HEADS),
        out_specs=_attn_out_spec(seq, tq, hb),
        scratch_shapes=[pltpu.VMEM((hb, VT_ROWS, seq), BF16),
                        pltpu.VMEM((hb, nb, HEAD_DIM), F32),
                        pltpu.VMEM((hb, nb, tq), F32)],
        compiler_params=_ATTN_PARAMS,
        name="moba_attention",
    )(jnp.asarray(ALIBI_MOBA), qkv, qkv, qkv)


def _gate_merge_kernel(h_ref, oa_ref, ob_ref, oc_ref, wga_ref, wgb_ref, wgc_ref,
                       ba_ref, bb_ref, bc_ref, wbr_ref, o_ref):
    h = h_ref[...]
    merged = None
    for n, (o_r, wg_r, b_r) in enumerate(((oa_ref, wga_ref, ba_ref), (ob_ref, wgb_ref, bb_ref),
                                          (oc_ref, wgc_ref, bc_ref))):
        gate = jax.nn.sigmoid(_dot(h, wg_r[...]) + b_r[...])
        term = gate * _dot(o_r[...], wbr_ref[n])
        merged = term if merged is None else merged + term
    o_ref[...] = merged.astype(o_ref.dtype)


def _gate_merge(h, oa, ob, oc, w_in, b_gate, w_branch, layer, gate_col0, *, tm, tn):
    m, d = h.shape
    tm, tn = _tile(m, tm), _tile(d, tn)
    nj = d // tn
    assert gate_col0 % tn == 0
    c0 = gate_col0 // tn
    o_spec = pl.BlockSpec((tm, MIX_W), lambda i, j: (i, 0))
    wg_specs = [pl.BlockSpec((None, d, tn),
                             functools.partial(lambda i, j, n: (layer, 0, c0 + n * nj + j), n=n))
                for n in range(N_BRANCH)]
    b_specs = [pl.BlockSpec((None, 1, tn),
                            functools.partial(lambda i, j, n: (layer, 0, n * nj + j), n=n))
               for n in range(N_BRANCH)]
    b_gate = b_gate.reshape(b_gate.shape[0], 1, -1)
    return pl.pallas_call(
        _gate_merge_kernel,
        out_shape=jax.ShapeDtypeStruct((m, d), BF16),
        grid=(m // tm, nj),
        in_specs=[pl.BlockSpec((tm, d), lambda i, j: (i, 0)), o_spec, o_spec, o_spec]
                 + wg_specs + b_specs
                 + [pl.BlockSpec((None, N_BRANCH, MIX_W, tn), lambda i, j: (layer, 0, 0, j))],
        out_specs=pl.BlockSpec((tm, tn), lambda i, j: (i, j)),
        compiler_params=pltpu.CompilerParams(
            dimension_semantics=("parallel", "arbitrary"),
            vmem_limit_bytes=V7X_VMEM_LIMIT_BYTES),
        name="gate_merge",
    )(h, oa, ob, oc, w_in, w_in, w_in, b_gate, b_gate, b_gate, w_branch)


def _out_proj_kernel(a_ref, w_ref, x_ref, o_ref):
    o_ref[...] = x_ref[...] + _dot(a_ref[...], w_ref[...])


def _out_proj(a, w, layer, x, *, tm, tn):
    m, d = x.shape
    k = a.shape[1]
    tm, tn = _tile(m, tm), _tile(d, tn)
    return pl.pallas_call(
        _out_proj_kernel,
        out_shape=jax.ShapeDtypeStruct((m, d), F32),
        grid=(m // tm, d // tn),
        in_specs=[pl.BlockSpec((tm, k), lambda i, j: (i, 0)),
                  pl.BlockSpec((None, k, tn), lambda i, j: (layer, 0, j)),
                  pl.BlockSpec((tm, tn), lambda i, j: (i, j))],
        out_specs=pl.BlockSpec((tm, tn), lambda i, j: (i, j)),
        compiler_params=pltpu.CompilerParams(
            dimension_semantics=("parallel", "arbitrary"),
            vmem_limit_bytes=V7X_VMEM_LIMIT_BYTES),
        name="out_proj_residual",
    )(a, w, x)


def _ffn_kernel(*refs, final_norm):
    if final_norm:
        x_ref, g_ref, wg_ref, wu_ref, wd_ref, fg_ref, o_ref, h_ref = refs
    else:
        x_ref, g_ref, wg_ref, wu_ref, wd_ref, o_ref, h_ref = refs
    j = pl.program_id(1)

    @pl.when(j == 0)
    def _():
        x = x_ref[...]
        h_ref[...] = _rms(x, g_ref[...]).astype(BF16)
        o_ref[...] = x

    h = h_ref[...]
    a = jax.nn.silu(_dot(h, wg_ref[...])) * _dot(h, wu_ref[...])
    o_ref[...] += _dot(a.astype(BF16), wd_ref[...])

    if final_norm:
        @pl.when(j == pl.num_programs(1) - 1)
        def _():
            o_ref[...] = _rms(o_ref[...], fg_ref[...])


def _ffn(x, g, wg, wu, wd, layer, final_g, *, tm, tf):
    m, d = x.shape
    dff = wg.shape[2]
    tm, tf = _tile(m, tm), _tile(dff, tf)
    vec = pl.BlockSpec((1, d), lambda i, j: (0, 0))
    in_specs = [
        pl.BlockSpec((tm, d), lambda i, j: (i, 0)),
        vec,
        pl.BlockSpec((None, d, tf), lambda i, j: (layer, 0, j)),
        pl.BlockSpec((None, d, tf), lambda i, j: (layer, 0, j)),
        pl.BlockSpec((None, tf, d), lambda i, j: (layer, j, 0)),
    ]
    args = [x, g.reshape(1, d), wg, wu, wd]
    if final_g is not None:
        in_specs.append(vec)
        args.append(final_g.reshape(1, d))
    return pl.pallas_call(
        functools.partial(_ffn_kernel, final_norm=final_g is not None),
        out_shape=jax.ShapeDtypeStruct((m, d), F32),
        grid=(m // tm, dff // tf),
        in_specs=in_specs,
        out_specs=pl.BlockSpec((tm, d), lambda i, j: (i, 0)),
        scratch_shapes=[pltpu.VMEM((tm, d), BF16)],
        compiler_params=pltpu.CompilerParams(
            dimension_semantics=("parallel", "arbitrary"),
            vmem_limit_bytes=V7X_VMEM_LIMIT_BYTES),
        name="swiglu_ffn",
    )(*args)


def kernel(x, norm_mix_g, norm_ffn_g, w_in, b_gate, lam_q1, lam_k1, lam_q2, lam_k2, subln_w,
           w_branch, w_out, w_ffn_gate, w_ffn_up, w_ffn_down, final_norm_g):
    batch, seq, d = x.shape
    depth = w_in.shape[0]
    n_qkv = 3 * N_BRANCH * MIX_W
    assert w_in.shape[2] == n_qkv + N_BRANCH * d
    xf = x.reshape(batch * seq, d)
    w_in, w_branch, w_out, w_ffn_gate, w_ffn_up, w_ffn_down = (
        w.astype(BF16) for w in (w_in, w_branch, w_out, w_ffn_gate, w_ffn_up, w_ffn_down))
    for l in range(depth):
        lambda_init = 0.8 - 0.6 * math.exp(-0.3 * l)
        qkv, h = _norm_matmul(xf, norm_mix_g[l], w_in, l, n_qkv, tm=1024, tn=2304,
                              name="norm_qkv_proj")
        oa = _diff_attention(qkv, lam_q1[l], lam_k1[l], lam_q2[l], lam_k2[l], subln_w[l],
                             lambda_init, batch, seq, tq=256, tk=256, hb=8)
        ob = _sb_attention(qkv, batch, seq, tq=256, tk=256, hb=8)
        oc = _moba_attention(qkv, batch, seq, tq=256, tk=256, hb=8)
        merged = _gate_merge(h, oa, ob, oc, w_in, b_gate, w_branch, l, n_qkv, tm=1024, tn=512)
        xf = _out_proj(merged, w_out, l, xf, tm=512, tn=2048)
        xf = _ffn(xf, norm_ffn_g[l], w_ffn_gate, w_ffn_up, w_ffn_down, l,
                  final_norm_g if l == depth - 1 else None, tm=1024, tf=512)
    return xf.reshape(batch, seq, d)
```

```python
import functools
import math

import jax
import jax.numpy as jnp
import numpy as np
from jax import lax
from jax.experimental import pallas as pl
from jax.experimental.pallas import tpu as pltpu

F32 = jnp.float32
BF16 = jnp.bfloat16

HEAD_DIM = 128
N_HEADS = 8
MIX_W = N_HEADS * HEAD_DIM
N_BRANCH = 3
MOBA_BLOCK = 256
MOBA_TOPK = 3
RMS_EPS = 1e-6
N_ALIBI = 2 * N_HEADS
_ALIBI_ALL = 2.0 ** (-8.0 * (np.arange(N_ALIBI) + 1) / N_ALIBI)
ALIBI_DIFF = _ALIBI_ALL[0::2].astype(np.float32)
ALIBI_MOBA = _ALIBI_ALL[1::2].astype(np.float32)

LOG2E = 1.4426950408889634
V7X_VMEM_LIMIT_BYTES = 56 * 1024 * 1024
MASKED = -1e30
VT_ROWS = HEAD_DIM + 16
EXP2_UNDERFLOW = -151.0


def _tile(n, target):
    best = None
    for c in range(128, min(n, target) + 1, 128):
        if n % c == 0:
            best = c
    assert best is not None, (n, target)
    return best


def _dot(a, b):
    return jnp.dot(a, b, preferred_element_type=F32)


def _dot_nt(a, b):
    return lax.dot_general(a, b, (((1,), (1,)), ((), ())), preferred_element_type=F32)


def _rms(x, g):
    ms = jnp.mean(x * x, axis=-1, keepdims=True)
    return x * lax.rsqrt(ms + RMS_EPS) * g


def _norm_matmul_kernel(x_ref, g_ref, w_ref, o_ref, h_ref):
    @pl.when(pl.program_id(1) == 0)
    def _():
        h_ref[...] = _rms(x_ref[...], g_ref[...]).astype(BF16)

    o_ref[...] = _dot(h_ref[...], w_ref[...]).astype(o_ref.dtype)


def _norm_matmul(x, g, w, layer, n, *, tm, tn, name):
    m, d = x.shape
    tm, tn = _tile(m, tm), _tile(n, tn)
    return pl.pallas_call(
        _norm_matmul_kernel,
        out_shape=(jax.ShapeDtypeStruct((m, n), BF16), jax.ShapeDtypeStruct((m, d), BF16)),
        grid=(m // tm, n // tn),
        in_specs=[
            pl.BlockSpec((tm, d), lambda i, j: (i, 0)),
            pl.BlockSpec((1, d), lambda i, j: (0, 0)),
            pl.BlockSpec((None, d, tn), lambda i, j: (layer, 0, j)),
        ],
        out_specs=(pl.BlockSpec((tm, tn), lambda i, j: (i, j)),
                   pl.BlockSpec((tm, d), lambda i, j: (i, 0))),
        compiler_params=pltpu.CompilerParams(
            dimension_semantics=("parallel", "arbitrary"),
            vmem_limit_bytes=V7X_VMEM_LIMIT_BYTES),
        name=name,
    )(x, g.reshape(1, d), w)


def _attn_specs(seq, tq, hb, col_q, col_k, col_v):
    nq = seq // tq
    w = hb * HEAD_DIM
    return [
        pl.BlockSpec((tq, w), lambda b, g, i: (b * nq + i, col_q // hb + g)),
        pl.BlockSpec((seq, w), lambda b, g, i: (b, col_k // hb + g)),
        pl.BlockSpec((seq, w), lambda b, g, i: (b, col_v // hb + g)),
    ]


def _attn_out_spec(seq, tq, hb):
    nq = seq // tq
    return pl.BlockSpec((tq, hb * HEAD_DIM), lambda b, g, i: (b * nq + i, g))


_ATTN_PARAMS = pltpu.CompilerParams(
    dimension_semantics=("parallel", "parallel", "arbitrary"),
    vmem_limit_bytes=V7X_VMEM_LIMIT_BYTES)


def _head(ref, s, rows=slice(None)):
    return ref[rows, s * HEAD_DIM:(s + 1) * HEAD_DIM]


def _transpose_v(v_ref, vt_ref, hb, ones_rows=False):
    for s in range(hb):
        vt_ref[s, :HEAD_DIM, :] = _head(v_ref, s).T
        if ones_rows:
            vt_ref[s, HEAD_DIM:, :] = jnp.ones((VT_ROWS - HEAD_DIM, vt_ref.shape[2]), vt_ref.dtype)


def _bias_columns(rows, slope):
    lane = lax.broadcasted_iota(jnp.int32, (rows, HEAD_DIM), 1)
    x = jnp.full((rows, HEAD_DIM), slope, F32)
    hi = x.astype(BF16).astype(F32)
    mid = (x - hi).astype(BF16).astype(F32)
    lo = x - hi - mid
    cols = jnp.where(lane == 0, hi, jnp.where(lane == 1, mid, jnp.where(lane == 2, lo, 0.0)))
    return cols.astype(BF16)


def _position_columns(tk):
    lane = lax.broadcasted_iota(jnp.int32, (tk, HEAD_DIM), 1)
    c = lax.broadcasted_iota(jnp.int32, (tk, HEAD_DIM), 0)
    return jnp.where(lane < 3, c, 0).astype(F32).astype(BF16)


def _key_query_index(tk, n, tq):
    c = lax.broadcasted_iota(jnp.int32, (tk, n), 0)
    r = lax.broadcasted_iota(jnp.int32, (tk, n), 1)
    if n != tq:
        r = r % tq
    return c, r


def _softmax_tiles(i, tq, tk, n, vt_ref, scores, stat_bias=None):
    n_full = i * (tq // tk)
    kidx, qidx = _key_query_index(tk, n, tq)
    heads = range(len(scores))

    def tiles(j, carry, diag):
        k0 = pl.multiple_of(j * tk, tk)
        us = [scores[s](k0) for s in heads]
        ps = []
        for s in heads:
            m, acc = carry[s]
            u, off = us[s]
            if diag is not None:
                u = jnp.where(kidx + diag * tk <= qidx, u, -jnp.inf)
            if stat_bias is not None:
                off = off + stat_bias[s](k0)
            m_new = jnp.maximum(m, jnp.max(u, axis=0, keepdims=True) + off)
            p = jnp.exp2(u - (m_new - off))
            ps.append((m_new, jnp.exp2(m - m_new), p.astype(BF16)))
        out = []
        for s in heads:
            m_new, alpha, p = ps[s]
            acc = alpha * carry[s][1] + _dot(vt_ref[s, :, pl.ds(k0, tk)], p)
            out.append((m_new, acc))
        return tuple(out)

    carry = tuple((jnp.full((1, n), -jnp.inf, F32), jnp.zeros((VT_ROWS, n), F32)) for _ in heads)
    for d in range(tq // tk):
        carry = tiles(n_full + d, carry, d)
    carry = lax.fori_loop(0, n_full, lambda j, c: tiles(j, c, None), carry)
    return [(acc[:HEAD_DIM], acc[HEAD_DIM:HEAD_DIM + 1]) for _, acc in carry]


def _diff_kernel(slopes_ref, lq1_ref, lk1_ref, lq2_ref, lk2_ref, subw_ref,
                 q_ref, k_ref, v_ref, o_ref, vt_ref, *, lambda_init, tq, tk, hb):
    g = pl.program_id(1)
    i = pl.program_id(2)

    @pl.when(i == 0)
    def _():
        _transpose_v(v_ref, vt_ref, hb, ones_rows=True)

    half = HEAD_DIM // 2
    n = 2 * tq
    pos_cols = _position_columns(tk)
    lane = lax.broadcasted_iota(jnp.int32, (tq, HEAD_DIM), 1)

    def make_scores(s):
        slope2 = slopes_ref[g * hb + s] * LOG2E
        q = _head(q_ref, s) * jnp.asarray(half ** -0.5, BF16)
        zero = jnp.zeros_like(q)
        qq = jnp.concatenate([jnp.where(lane < half, q, zero), jnp.where(lane < half, zero, q)],
                             axis=0)
        qq = jnp.concatenate([qq, _bias_columns(n, slopes_ref[g * hb + s])], axis=1)

        def scores(k0):
            ka = jnp.concatenate([_head(k_ref, s, pl.ds(k0, tk)), pos_cols], axis=1)
            return _dot_nt(ka, qq) * LOG2E, slope2 * (k0 - i * tq).astype(F32)
        return scores

    outs = _softmax_tiles(i, tq, tk, n, vt_ref, [make_scores(s) for s in range(hb)])
    lam = (jnp.exp(jnp.sum(lq1_ref[...] * lk1_ref[...], axis=1, keepdims=True))
           - jnp.exp(jnp.sum(lq2_ref[...] * lk2_ref[...], axis=1, keepdims=True))
           + lambda_init)
    for s, (acc, l) in enumerate(outs):
        o = acc / l
        o = (o[:, :tq] - lam * o[:, tq:]).T
        o = _rms(o, subw_ref[...]) * (1.0 - lambda_init)
        o_ref[:, s * HEAD_DIM:(s + 1) * HEAD_DIM] = o.astype(o_ref.dtype)


def _diff_attention(qkv, lq1, lk1, lq2, lk2, subw, lambda_init, batch, seq, *, tq, tk, hb):
    half = HEAD_DIM // 2
    small = pl.BlockSpec((1, half), lambda b, g, i: (0, 0))
    return pl.pallas_call(
        functools.partial(_diff_kernel, lambda_init=lambda_init, tq=tq, tk=tk, hb=hb),
        out_shape=jax.ShapeDtypeStruct((batch * seq, MIX_W), BF16),
        grid=(batch, N_HEADS // hb, seq // tq),
        in_specs=[pl.BlockSpec(memory_space=pltpu.SMEM), small, small, small, small,
                  pl.BlockSpec((1, HEAD_DIM), lambda b, g, i: (0, 0))]
                 + _attn_specs(seq, tq, hb, 0, N_HEADS, 2 * N_HEADS),
        out_specs=_attn_out_spec(seq, tq, hb),
        scratch_shapes=[pltpu.VMEM((hb, VT_ROWS, seq), BF16)],
        compiler_params=_ATTN_PARAMS,
        name="diff_attention",
    )(jnp.asarray(ALIBI_DIFF), lq1.reshape(1, half), lk1.reshape(1, half),
      lq2.reshape(1, half), lk2.reshape(1, half), subw.reshape(1, HEAD_DIM), qkv, qkv, qkv)


def _sb_kernel(q_ref, k_ref, v_ref, o_ref, vt_ref, *, tq, tk, hb):
    i = pl.program_id(2)

    @pl.when(i == 0)
    def _():
        _transpose_v(v_ref, vt_ref, hb)

    scale2 = HEAD_DIM ** -0.5 * LOG2E
    n_full = i * (tq // tk)
    kidx, qidx = _key_query_index(tk, tq, tq)
    a = lax.broadcasted_iota(jnp.int32, (tk + 16, tk), 0)
    b = lax.broadcasted_iota(jnp.int32, (tk + 16, tk), 1)
    later = jnp.where((b > a) | (a >= tk), 1.0, 0.0).astype(BF16)
    qs = [_head(q_ref, s) for s in range(hb)]
    heads = range(hb)

    def tiles(j, carry, diag):
        masked = diag is not None
        k0 = pl.multiple_of(j * tk, tk)
        zs = [_dot_nt(_head(k_ref, s, pl.ds(k0, tk)), qs[s]) * scale2 for s in heads]
        if masked:
            strict = kidx + diag * tk < qidx
        mid = []
        for s in heads:
            z = zs[s]
            sp = jnp.log2(1.0 + jnp.exp2(-jnp.abs(z)))
            log_beta = jnp.minimum(z, 0.0) - sp
            log_fail = -jnp.maximum(z, 0.0) - sp
            if masked:
                log_fail = jnp.where(strict, log_fail, 0.0)
            hi = log_fail.astype(BF16)
            lo = (log_fail - hi.astype(F32)).astype(BF16)
            mid.append((log_beta, hi, lo))
        sufs = [_dot(later, mid[s][1]) + _dot(later, mid[s][2]) for s in heads]
        ws = []
        for s in heads:
            w = jnp.exp2(mid[s][0] + sufs[s][:tk] + carry[s][1])
            if masked:
                w = jnp.where(strict, w, 0.0)
            ws.append(w.astype(BF16))
        out = []
        for s in heads:
            acc = carry[s][0] + _dot(vt_ref[s, :, pl.ds(k0, tk)], ws[s])
            run = carry[s][1] + sufs[s][tk:tk + 1]
            out.append((acc, run))
        return tuple(out)

    def alive(carry):
        top = carry[0][1]
        for s in heads[1:]:
            top = jnp.maximum(top, carry[s][1])
        return jnp.max(top) >= EXP2_UNDERFLOW

    carry = tuple((jnp.zeros((HEAD_DIM, tq), F32), jnp.zeros((1, tq), F32)) for _ in heads)
    for d in reversed(range(tq // tk)):
        carry = tiles(n_full + d, carry, d)

    def body(state):
        t, _, c = state
        c = tiles(n_full - 1 - t, c, None)
        return t + 1, alive(c), c

    _, _, carry = lax.while_loop(lambda st: jnp.logical_and(st[0] < n_full, st[1]), body,
                                 (jnp.int32(0), alive(carry), carry))
    for s in heads:
        o_ref[:, s * HEAD_DIM:(s + 1) * HEAD_DIM] = carry[s][0].T.astype(o_ref.dtype)


def _sb_attention(qkv, batch, seq, *, tq, tk, hb):
    return pl.pallas_call(
        functools.partial(_sb_kernel, tq=tq, tk=tk, hb=hb),
        out_shape=jax.ShapeDtypeStruct((batch * seq, MIX_W), BF16),
        grid=(batch, N_HEADS // hb, seq // tq),
        in_specs=_attn_specs(seq, tq, hb, 3 * N_HEADS, 4 * N_HEADS, 5 * N_HEADS),
        out_specs=_attn_out_spec(seq, tq, hb),
        scratch_shapes=[pltpu.VMEM((hb, HEAD_DIM, seq), BF16)],
        compiler_params=_ATTN_PARAMS,
        name="stick_breaking_attention",
    )(qkv, qkv, qkv)


def _moba_kernel(slopes_ref, q_ref, k_ref, v_ref, o_ref, vt_ref, kmean_ref, selb_ref,
                 *, tq, tk, hb):
    nb = k_ref.shape[0] // MOBA_BLOCK
    g = pl.program_id(1)
    i = pl.program_id(2)
    own = (i * tq) // MOBA_BLOCK

    @pl.when(i == 0)
    def _():
        _transpose_v(v_ref, vt_ref, hb, ones_rows=True)
        blk = lax.broadcasted_iota(jnp.int32, (nb, k_ref.shape[0]), 0)
        pos = lax.broadcasted_iota(jnp.int32, (nb, k_ref.shape[0]), 1)
        member = jnp.where((pos >= blk * MOBA_BLOCK) & (pos < (blk + 1) * MOBA_BLOCK), 1.0, 0.0)
        member = member.astype(BF16)
        for s in range(hb):
            kmean_ref[s] = _dot(member, _head(k_ref, s)) * (1.0 / MOBA_BLOCK)

    c1 = HEAD_DIM ** -0.5 * LOG2E
    n_idx = lax.broadcasted_iota(jnp.int32, (nb, tq), 0)
    pos_cols = _position_columns(tk)

    def make_head(s):
        slope2 = slopes_ref[g * hb + s] * LOG2E
        q = _head(q_ref, s)
        km = kmean_ref[s]
        km_hi = km.astype(BF16)
        km_lo = (km - km_hi.astype(F32)).astype(BF16)
        gate = _dot_nt(km_hi, q) + _dot_nt(km_lo, q)
        beaten = jnp.zeros((nb, tq), jnp.int32)
        for mth in range(nb):
            gm = gate[mth:mth + 1, :]
            beats = (gm > gate) | ((gm == gate) & (mth < n_idx))
            beaten = beaten + jnp.where(beats & (mth < own), 1, 0)
        visible = (n_idx == own) | ((n_idx < own) & (beaten < MOBA_TOPK))
        selb_ref[s] = jnp.where(visible, 0.0, MASKED)
        qa = jnp.concatenate([q, _bias_columns(tq, slopes_ref[g * hb + s] * HEAD_DIM ** 0.5)], axis=1)

        def scores(k0):
            ka = jnp.concatenate([_head(k_ref, s, pl.ds(k0, tk)), pos_cols], axis=1)
            return _dot_nt(ka, qa) * c1, slope2 * (k0 - i * tq).astype(F32)

        def block_bias(k0):
            return selb_ref[s, pl.ds(k0 // MOBA_BLOCK, 1), :]
        return scores, block_bias

    fns = [make_head(s) for s in range(hb)]
    outs = _softmax_tiles(i, tq, tk, tq, vt_ref, [f[0] for f in fns], [f[1] for f in fns])
    for s, (acc, l) in enumerate(outs):
        o_ref[:, s * HEAD_DIM:(s + 1) * HEAD_DIM] = (acc / l).T.astype(o_ref.dtype)


def _moba_attention(qkv, batch, seq, *, tq, tk, hb):
    assert seq % MOBA_BLOCK == 0 and MOBA_BLOCK % tk == 0 and MOBA_BLOCK % tq == 0
    nb = seq // MOBA_BLOCK
    return pl.pallas_call(
        functools.partial(_moba_kernel, tq=tq, tk=tk, hb=hb),
        out_shape=jax.ShapeDtypeStruct((batch * seq, MIX_W), BF16),
        grid=(batch, N_HEADS // hb, seq // tq),
        in_specs=[pl.BlockSpec(memory_space=pltpu.SMEM)]
                 + _attn_specs(seq, tq, hb, 6 * N_HEADS, 7 * N_HEADS, 8 * N_HEADS),
        out_specs=_attn_out_spec(seq, tq, hb),
        scratch_shapes=[pltpu.VMEM((hb, VT_ROWS, seq), BF16),
                        pltpu.VMEM((hb, nb, HEAD_DIM), F32),
                        pltpu.VMEM((hb, nb, tq), F32)],
        compiler_params=_ATTN_PARAMS,
        name="moba_attention",
    )(jnp.asarray(ALIBI_MOBA), qkv, qkv, qkv)


def _gate_merge_kernel(h_ref, oa_ref, ob_ref, oc_ref, wga_ref, wgb_ref, wgc_ref,
                       ba_ref, bb_ref, bc_ref, wbr_ref, o_ref):
    h = h_ref[...]
    merged = None
    for n, (o_r, wg_r, b_r) in enumerate(((oa_ref, wga_ref, ba_ref), (ob_ref, wgb_ref, bb_ref),
                                          (oc_ref, wgc_ref, bc_ref))):
        gate = jax.nn.sigmoid(_dot(h, wg_r[...]) + b_r[...])
        term = gate * _dot(o_r[...], wbr_ref[n])
        merged = term if merged is None else merged + term
    o_ref[...] = merged.astype(o_ref.dtype)


def _gate_merge(h, oa, ob, oc, w_in, b_gate, w_branch, layer, gate_col0, *, tm, tn):
    m, d = h.shape
    tm, tn = _tile(m, tm), _tile(d, tn)
    nj = d // tn
    assert gate_col0 % tn == 0
    c0 = gate_col0 // tn
    o_spec = pl.BlockSpec((tm, MIX_W), lambda i, j: (i, 0))
    wg_specs = [pl.BlockSpec((None, d, tn),
                             functools.partial(lambda i, j, n: (layer, 0, c0 + n * nj + j), n=n))
                for n in range(N_BRANCH)]
    b_specs = [pl.BlockSpec((None, 1, tn),
                            functools.partial(lambda i, j, n: (layer, 0, n * nj + j), n=n))
               for n in range(N_BRANCH)]
    b_gate = b_gate.reshape(b_gate.shape[0], 1, -1)
    return pl.pallas_call(
        _gate_merge_kernel,
        out_shape=jax.ShapeDtypeStruct((m, d), BF16),
        grid=(m // tm, nj),
        in_specs=[pl.BlockSpec((tm, d), lambda i, j: (i, 0)), o_spec, o_spec, o_spec]
                 + wg_specs + b_specs
                 + [pl.BlockSpec((None, N_BRANCH, MIX_W, tn), lambda i, j: (layer, 0, 0, j))],
        out_specs=pl.BlockSpec((tm, tn), lambda i, j: (i, j)),
        compiler_params=pltpu.CompilerParams(
            dimension_semantics=("parallel", "arbitrary"),
            vmem_limit_bytes=V7X_VMEM_LIMIT_BYTES),
        name="gate_merge",
    )(h, oa, ob, oc, w_in, w_in, w_in, b_gate, b_gate, b_gate, w_branch)


def _out_proj_kernel(a_ref, w_ref, x_ref, o_ref):
    o_ref[...] = x_ref[...] + _dot(a_ref[...], w_ref[...])


def _out_proj(a, w, layer, x, *, tm, tn):
    m, d = x.shape
    k = a.shape[1]
    tm, tn = _tile(m, tm), _tile(d, tn)
    return pl.pallas_call(
        _out_proj_kernel,
        out_shape=jax.ShapeDtypeStruct((m, d), F32),
        grid=(m // tm, d // tn),
        in_specs=[pl.BlockSpec((tm, k), lambda i, j: (i, 0)),
                  pl.BlockSpec((None, k, tn), lambda i, j: (layer, 0, j)),
                  pl.BlockSpec((tm, tn), lambda i, j: (i, j))],
        out_specs=pl.BlockSpec((tm, tn), lambda i, j: (i, j)),
        compiler_params=pltpu.CompilerParams(
            dimension_semantics=("parallel", "arbitrary"),
            vmem_limit_bytes=V7X_VMEM_LIMIT_BYTES),
        name="out_proj_residual",
    )(a, w, x)


def _ffn_kernel(*refs, final_norm):
    if final_norm:
        x_ref, g_ref, wg_ref, wu_ref, wd_ref, fg_ref, o_ref, h_ref = refs
    else:
        x_ref, g_ref, wg_ref, wu_ref, wd_ref, o_ref, h_ref = refs
    j = pl.program_id(1)

    @pl.when(j == 0)
    def _():
        x = x_ref[...]
        h_ref[...] = _rms(x, g_ref[...]).astype(BF16)
        o_ref[...] = x

    h = h_ref[...]
    a = jax.nn.silu(_dot(h, wg_ref[...])) * _dot(h, wu_ref[...])
    o_ref[...] += _dot(a.astype(BF16), wd_ref[...])

    if final_norm:
        @pl.when(j == pl.num_programs(1) - 1)
        def _():
            o_ref[...] = _rms(o_ref[...], fg_ref[...])


def _ffn(x, g, wg, wu, wd, layer, final_g, *, tm, tf):
    m, d = x.shape
    dff = wg.shape[2]
    tm, tf = _tile(m, tm), _tile(dff, tf)
    vec = pl.BlockSpec((1, d), lambda i, j: (0, 0))
    in_specs = [
        pl.BlockSpec((tm, d), lambda i, j: (i, 0)),
        vec,
        pl.BlockSpec((None, d, tf), lambda i, j: (layer, 0, j)),
        pl.BlockSpec((None, d, tf), lambda i, j: (layer, 0, j)),
        pl.BlockSpec((None, tf, d), lambda i, j: (layer, j, 0)),
    ]
    args = [x, g.reshape(1, d), wg, wu, wd]
    if final_g is not None:
        in_specs.append(vec)
        args.append(final_g.reshape(1, d))
    return pl.pallas_call(
        functools.partial(_ffn_kernel, final_norm=final_g is not None),
        out_shape=jax.ShapeDtypeStruct((m, d), F32),
        grid=(m // tm, dff // tf),
        in_specs=in_specs,
        out_specs=pl.BlockSpec((tm, d), lambda i, j: (i, 0)),
        scratch_shapes=[pltpu.VMEM((tm, d), BF16)],
        compiler_params=pltpu.CompilerParams(
            dimension_semantics=("parallel", "arbitrary"),
            vmem_limit_bytes=V7X_VMEM_LIMIT_BYTES),
        name="swiglu_ffn",
    )(*args)


def kernel(x, norm_mix_g, norm_ffn_g, w_in, b_gate, lam_q1, lam_k1, lam_q2, lam_k2, subln_w,
           w_branch, w_out, w_ffn_gate, w_ffn_up, w_ffn_down, final_norm_g):
    batch, seq, d = x.shape
    depth = w_in.shape[0]
    n_qkv = 3 * N_BRANCH * MIX_W
    assert w_in.shape[2] == n_qkv + N_BRANCH * d
    xf = x.reshape(batch * seq, d)
    w_in, w_branch, w_out, w_ffn_gate, w_ffn_up, w_ffn_down = (
        w.astype(BF16) for w in (w_in, w_branch, w_out, w_ffn_gate, w_ffn_up, w_ffn_down))
    for l in range(depth):
        lambda_init = 0.8 - 0.6 * math.exp(-0.3 * l)
        qkv, h = _norm_matmul(xf, norm_mix_g[l], w_in, l, n_qkv, tm=1024, tn=2304,
                              name="norm_qkv_proj")
        oa = _diff_attention(qkv, lam_q1[l], lam_k1[l], lam_q2[l], lam_k2[l], subln_w[l],
                             lambda_init, batch, seq, tq=256, tk=256, hb=8)
        ob = _sb_attention(qkv, batch, seq, tq=256, tk=256, hb=8)
        oc = _moba_attention(qkv, batch, seq, tq=256, tk=256, hb=8)
        merged = _gate_merge(h, oa, ob, oc, w_in, b_gate, w_branch, l, n_qkv, tm=1024, tn=512)
        xf = _out_proj(merged, w_out, l, xf, tm=512, tn=2048)
        xf = _ffn(xf, norm_ffn_g[l], w_ffn_gate, w_ffn_up, w_ffn_down, l,
                  final_norm_g if l == depth - 1 else None, tm=1024, tf=512)
    return xf.reshape(batch, seq, d)
```
